```python
import jax, jax.numpy as jnp
from jax import lax
import numpy as np

D_MODEL = 2048
BATCH = 2
SEQ = 16384
DEPTH = 2
DEC_BATCH = 4
DEC_SEQ = 4096
PAST_LEN = 128

MLA_HEADS = D_MODEL // 128
Q_LORA_RANK = 3 * D_MODEL // 8
KV_LORA_RANK = D_MODEL // 4
NOPE_DIM = 128
ROPE_DIM = 64
V_DIM = 128
ROPE_THETA = 10000.0
Q_BLOCK = 128
GMLP_WIDTH = D_MODEL
CHUNK = 128
GMLP_GROUPS = GMLP_WIDTH // 128
LRU_WIDTH = D_MODEL
LRU_BLOCKS = LRU_WIDTH // 128
LRU_BLOCK_DIM = LRU_WIDTH // LRU_BLOCKS
CONV_W = 4
LRU_C = 8.0
N_BRANCH = 3
BRANCH_WIDTH = D_MODEL
IN_SPLITS = (Q_LORA_RANK, KV_LORA_RANK, ROPE_DIM, 2 * GMLP_WIDTH, LRU_WIDTH, LRU_WIDTH, N_BRANCH * D_MODEL)
IN_COLS = Q_LORA_RANK + KV_LORA_RANK + ROPE_DIM + 2 * GMLP_WIDTH + 2 * LRU_WIDTH + N_BRANCH * D_MODEL
N_EXPERTS = 256
TOP_K = 8
N_GROUPS = 8
TOPK_GROUPS = 4
D_EXPERT = D_MODEL // 4
D_SHARED = D_EXPERT
ROUTED_SCALE = 2.5
EXPERT_BLOCK = 128
ALPHA = (2 * DEPTH) ** 0.25
BETA = (8 * DEPTH) ** -0.25
LN_EPS = 1e-5
RMS_EPS = 1e-6

kernel_name = "hybrid_mla_sgu_rglru_moe_encoder"


def layer_norm(x, g=None, b=None):
    xf = x.astype(jnp.float32)
    mu = jnp.mean(xf, axis=-1, keepdims=True)
    var = jnp.mean(jnp.square(xf - mu), axis=-1, keepdims=True)
    y = (xf - mu) * lax.rsqrt(var + LN_EPS)
    if g is not None:
        y = y * g.astype(jnp.float32) + b.astype(jnp.float32)
    return y.astype(x.dtype)


def rms_norm(x, g):
    xf = x.astype(jnp.float32)
    y = xf * lax.rsqrt(jnp.mean(jnp.square(xf), axis=-1, keepdims=True) + RMS_EPS)
    return (y * g.astype(jnp.float32)).astype(x.dtype)


def rope_tables(S):
    half = ROPE_DIM // 2
    inv_freq = ROPE_THETA ** (-jnp.arange(half, dtype=jnp.float32) / half)
    ang = jnp.arange(S, dtype=jnp.float32)[:, None] * inv_freq[None, :]
    return jnp.cos(ang), jnp.sin(ang)


def apply_rope(x, cos, sin):
    x1, x2 = jnp.split(x, 2, axis=-1)
    cos = cos.astype(x.dtype)
    sin = sin.astype(x.dtype)
    return jnp.concatenate([x1 * cos - x2 * sin, x2 * cos + x1 * sin], axis=-1)


def dense_attention(q_nope, q_rope, k_nope, k_rope, v):
    B, S, H, _ = q_nope.shape
    n_blk = S // Q_BLOCK
    scale = (NOPE_DIM + ROPE_DIM) ** -0.5

    def one_block(i):
        start = i * Q_BLOCK
        qn = lax.dynamic_slice_in_dim(q_nope, start, Q_BLOCK, axis=1)
        qr = lax.dynamic_slice_in_dim(q_rope, start, Q_BLOCK, axis=1)
        s = jnp.einsum('bqhd,bkhd->bhqk', qn, k_nope) + jnp.einsum('bqhr,bkr->bhqk', qr, k_rope)
        p = jax.nn.softmax(s.astype(jnp.float32) * scale, axis=-1).astype(v.dtype)
        return jnp.einsum('bhqk,bkhd->bqhd', p, v)

    out = lax.map(one_block, jnp.arange(n_blk))
    return jnp.moveaxis(out, 0, 1).reshape(B, S, H * V_DIM)


def mla_branch(q_lat, kv_lat, k_rope, q_norm_g, w_uq, kv_norm_g, w_ukv):
    B, S, _ = q_lat.shape
    q = jnp.einsum('bsr,rc->bsc', rms_norm(q_lat, q_norm_g), w_uq).reshape(B, S, MLA_HEADS, NOPE_DIM + ROPE_DIM)
    kv = jnp.einsum('bsr,rc->bsc', rms_norm(kv_lat, kv_norm_g), w_ukv).reshape(B, S, MLA_HEADS, NOPE_DIM + V_DIM)
    q_nope, q_rope = q[..., :NOPE_DIM], q[..., NOPE_DIM:]
    k_nope, v = kv[..., :NOPE_DIM], kv[..., NOPE_DIM:]
    cos, sin = rope_tables(S)
    q_rope = apply_rope(q_rope, cos[None, :, None, :], sin[None, :, None, :])
    k_rope = apply_rope(k_rope, cos[None], sin[None])
    return dense_attention(q_nope, q_rope, k_nope, k_rope, v)


def sgu_branch(gm, norm_g, norm_b, w_spatial, b_spatial):
    B, S, _ = gm.shape
    u, v = jnp.split(jax.nn.gelu(gm, approximate=False), 2, axis=-1)
    v = layer_norm(v, norm_g, norm_b).reshape(B, S // CHUNK, CHUNK, GMLP_GROUPS, GMLP_WIDTH // GMLP_GROUPS)
    v = jnp.einsum('gpq,bnqgc->bnpgc', w_spatial, v) + b_spatial.T[None, None, :, :, None]
    return u * v.reshape(B, S, GMLP_WIDTH)


def rglru_scan(x, w_r, b_r, w_i, b_i, logit):
    B, S, W = x.shape
    xb = x.reshape(B, S, LRU_BLOCKS, LRU_BLOCK_DIM)
    r = jax.nn.sigmoid(jnp.einsum('bsnc,ncd->bsnd', xb, w_r).reshape(B, S, W) + b_r)
    i = jax.nn.sigmoid(jnp.einsum('bsnc,ncd->bsnd', xb, w_i).reshape(B, S, W) + b_i)
    log_a = -LRU_C * r.astype(jnp.float32) * jax.nn.softplus(-logit.astype(jnp.float32))
    a = jnp.exp(log_a)
    mult = jnp.sqrt(-jnp.expm1(2.0 * log_a)).at[:, 0].set(1.0)
    u = mult * (i * x).astype(jnp.float32)

    def combine(lhs, rhs):
        a1, b1 = lhs
        a2, b2 = rhs
        return a1 * a2, a2 * b1 + b2

    _, h = lax.associative_scan(combine, (a, u), axis=1)
    return h.astype(x.dtype)


def rglru_branch(rec_x, rec_g, conv_w, conv_b, w_rgate, b_rgate, w_igate, b_igate, lru_logit):
    B, S, _ = rec_x.shape
    xp = jnp.pad(rec_x, ((0, 0), (CONV_W // 2, CONV_W - 1 - CONV_W // 2), (0, 0)))
    xc = conv_b + xp[:, 0:S] * conv_w[0]
    for j in range(1, CONV_W):
        xc = xc + xp[:, j:j + S] * conv_w[j]
    h_fwd = rglru_scan(xc, w_rgate[0], b_rgate[0], w_igate[0], b_igate[0], lru_logit[0])
    h_bwd = jnp.flip(rglru_scan(jnp.flip(xc, 1), w_rgate[1], b_rgate[1], w_igate[1], b_igate[1], lru_logit[1]), 1)
    return (h_fwd + h_bwd) * jax.nn.gelu(rec_g)


def token_mixer(h, w_in, q_norm_g, w_uq, kv_norm_g, w_ukv, sgu_norm_g, sgu_norm_b, w_spatial, b_spatial,
                conv_w, conv_b, w_rgate, b_rgate, w_igate, b_igate, lru_logit, w_branch, w_out):
    B, S, _ = h.shape
    proj = jnp.einsum('bsd,dc->bsc', h, w_in)
    split_points = np.cumsum(IN_SPLITS)[:-1].tolist()
    q_lat, kv_lat, k_rope, gm, rec_x, rec_g, gate_logits = jnp.split(proj, split_points, axis=-1)
    o_a = mla_branch(q_lat, kv_lat, k_rope, q_norm_g, w_uq, kv_norm_g, w_ukv)
    o_b = sgu_branch(gm, sgu_norm_g, sgu_norm_b, w_spatial, b_spatial)
    o_c = rglru_branch(rec_x, rec_g, conv_w, conv_b, w_rgate, b_rgate, w_igate, b_igate, lru_logit)
    o = jnp.stack([o_a, o_b, o_c], axis=2)
    p = jnp.einsum('bsnw,nwd->bsnd', o, w_branch)
    g = jax.nn.sigmoid(gate_logits.reshape(B, S, N_BRANCH, D_MODEL))
    merged = jnp.sum(g * p, axis=2)
    return jnp.einsum('bsd,de->bse', merged, w_out)


def route(xt, w_router, router_bias):
    T = xt.shape[0]
    scores = jax.nn.sigmoid(jnp.einsum('td,de->te', xt, w_router).astype(jnp.float32))
    biased = scores + router_bias.astype(jnp.float32)
    grp = biased.reshape(T, N_GROUPS, N_EXPERTS // N_GROUPS)
    grp_score = jnp.sum(lax.top_k(grp, 2)[0], axis=-1)
    _, top_grp = lax.top_k(grp_score, TOPK_GROUPS)
    grp_keep = jnp.any(top_grp[:, :, None] == jnp.arange(N_GROUPS)[None, None, :], axis=1)
    masked = jnp.where(grp_keep[:, :, None], grp, -jnp.inf).reshape(T, N_EXPERTS)
    _, top_idx = lax.top_k(masked, TOP_K)
    w = jnp.take_along_axis(scores, top_idx, axis=-1)
    w = w / jnp.sum(w, axis=-1, keepdims=True) * ROUTED_SCALE
    return top_idx, w


def routed_experts(xt, top_idx, top_w, w_gate, w_up, w_down):
    n_tok, D = xt.shape
    n_pairs = n_tok * TOP_K
    pair_expert = top_idx.reshape(n_pairs)
    pair_token = jnp.arange(n_pairs, dtype=jnp.int32) // TOP_K
    pair_weight = top_w.reshape(n_pairs)
    order = jnp.argsort(pair_expert)
    sorted_expert = pair_expert[order]
    counts = jnp.bincount(pair_expert, length=N_EXPERTS)
    padded = (counts + EXPERT_BLOCK - 1) // EXPERT_BLOCK * EXPERT_BLOCK
    padded_end = jnp.cumsum(padded)
    padded_start = padded_end - padded
    start = jnp.cumsum(counts) - counts
    slot = padded_start[sorted_expert] + (jnp.arange(n_pairs, dtype=jnp.int32) - start[sorted_expert])
    n_blocks = -(-n_pairs // EXPERT_BLOCK) + N_EXPERTS
    n_slots = n_blocks * EXPERT_BLOCK
    slot_token = jnp.full((n_slots,), n_tok, jnp.int32).at[slot].set(pair_token[order])
    slot_weight = jnp.zeros((n_slots,), xt.dtype).at[slot].set(pair_weight[order])
    block_start = jnp.arange(n_blocks, dtype=jnp.int32) * EXPERT_BLOCK
    block_expert = jnp.minimum(jnp.searchsorted(padded_end, block_start, side='right'), N_EXPERTS - 1)
    x_pad = jnp.concatenate([xt, jnp.zeros((1, D), xt.dtype)], axis=0)

    def step(acc, blk):
        tok, wt, e = blk
        xb = x_pad[tok]
        hb = jax.nn.silu(xb @ w_gate[e]) * (xb @ w_up[e])
        yb = (hb @ w_down[e]) * wt[:, None]
        return acc.at[tok].add(yb), None

    acc, _ = lax.scan(step, jnp.zeros_like(x_pad),
                      (slot_token.reshape(n_blocks, EXPERT_BLOCK), slot_weight.reshape(n_blocks, EXPERT_BLOCK), block_expert))
    return acc[:n_tok]


def moe(h, w_router, router_bias, w_e_gate, w_e_up, w_e_down, w_sh_gate, w_sh_up, w_sh_down):
    B, S, D = h.shape
    xt = h.reshape(B * S, D)
    top_idx, top_w = route(xt, w_router, router_bias)
    routed = routed_experts(xt, top_idx, top_w.astype(xt.dtype), w_e_gate, w_e_up, w_e_down)
    shared = jnp.einsum('tf,fd->td', jax.nn.silu(xt @ w_sh_gate) * (xt @ w_sh_up), w_sh_down)
    return (routed + shared).reshape(B, S, D)


def encoder_trunk(x, c, weights):
    (w_mod, b_mod, w_in, q_norm_g, w_uq, kv_norm_g, w_ukv, sgu_norm_g, sgu_norm_b, w_spatial, b_spatial,
     conv_w, conv_b, w_rgate, b_rgate, w_igate, b_igate, lru_logit, w_branch, w_out, ln1_g, ln1_b,
     w_router, router_bias, w_e_gate, w_e_up, w_e_down, w_sh_gate, w_sh_up, w_sh_down, ln2_g, ln2_b) = weights
    for l in range(DEPTH):
        mod = jnp.einsum('bc,cm->bm', jax.nn.silu(c), w_mod[l]) + b_mod[l]
        sh_a, sc_a, g_a, sh_m, sc_m, g_m = jnp.split(mod[:, None, :], 6, axis=-1)
        h = layer_norm(x) * (1.0 + sc_a) + sh_a
        y = token_mixer(h, w_in[l], q_norm_g[l], w_uq[l], kv_norm_g[l], w_ukv[l], sgu_norm_g[l], sgu_norm_b[l],
                        w_spatial[l], b_spatial[l], conv_w[l], conv_b[l], w_rgate[l], b_rgate[l],
                        w_igate[l], b_igate[l], lru_logit[l], w_branch[l], w_out[l])
        x = layer_norm(ALPHA * x + g_a * y, ln1_g[l], ln1_b[l])
        h = layer_norm(x) * (1.0 + sc_m) + sh_m
        y = moe(h, w_router[l], router_bias[l], w_e_gate[l], w_e_up[l], w_e_down[l],
                w_sh_gate[l], w_sh_up[l], w_sh_down[l])
        x = layer_norm(ALPHA * x + g_m * y, ln2_g[l], ln2_b[l])
    return x


def setup_inputs(seed: int = 0) -> dict:
    key = jax.random.key(seed)
    keys = iter(list(jax.random.split(key, 64)))
    L = DEPTH

    def nrm(shape, scale):
        return jax.random.normal(next(keys), shape, jnp.float32) * scale

    def gain(shape):
        return 1.0 + 0.01 * jax.random.normal(next(keys), shape, jnp.float32)

    a0 = jax.random.uniform(next(keys), (L, 2, LRU_WIDTH), jnp.float32, 0.9, 0.999)
    lru_logit = jnp.log(a0) - jnp.log1p(-a0)
    return {
        "x_prompt": nrm((BATCH, SEQ, D_MODEL), 1.0),
        "x_sample": nrm((DEC_BATCH, DEC_SEQ, D_MODEL), 1.0),
        "c_prompt": nrm((BATCH, D_MODEL), 1.0),
        "c_sample": nrm((DEC_BATCH, D_MODEL), 1.0),
        "w_mod": nrm((L, D_MODEL, 6 * D_MODEL), 0.5 * D_MODEL ** -0.5),
        "b_mod": nrm((L, 6 * D_MODEL), 0.01),
        "w_in": nrm((L, D_MODEL, IN_COLS), D_MODEL ** -0.5),
        "q_norm_g": gain((L, Q_LORA_RANK)),
        "w_uq": nrm((L, Q_LORA_RANK, MLA_HEADS * (NOPE_DIM + ROPE_DIM)), Q_LORA_RANK ** -0.5),
        "kv_norm_g": gain((L, KV_LORA_RANK)),
        "w_ukv": nrm((L, KV_LORA_RANK, MLA_HEADS * (NOPE_DIM + V_DIM)), KV_LORA_RANK ** -0.5),
        "sgu_norm_g": gain((L, GMLP_WIDTH)),
        "sgu_norm_b": nrm((L, GMLP_WIDTH), 0.01),
        "w_spatial": nrm((L, GMLP_GROUPS, CHUNK, CHUNK), CHUNK ** -0.5),
        "b_spatial": gain((L, GMLP_GROUPS, CHUNK)),
        "conv_w": nrm((L, CONV_W, LRU_WIDTH), CONV_W ** -0.5),
        "conv_b": nrm((L, LRU_WIDTH), 0.01),
        "w_rgate": nrm((L, 2, LRU_BLOCKS, LRU_BLOCK_DIM, LRU_BLOCK_DIM), LRU_BLOCK_DIM ** -0.5),
        "b_rgate": nrm((L, 2, LRU_WIDTH), 0.01),
        "w_igate": nrm((L, 2, LRU_BLOCKS, LRU_BLOCK_DIM, LRU_BLOCK_DIM), LRU_BLOCK_DIM ** -0.5),
        "b_igate": nrm((L, 2, LRU_WIDTH), 0.01),
        "lru_logit": lru_logit,
        "w_branch": nrm((L, N_BRANCH, BRANCH_WIDTH, D_MODEL), BETA * BRANCH_WIDTH ** -0.5),
        "w_out": nrm((L, D_MODEL, D_MODEL), BETA * D_MODEL ** -0.5),
        "ln1_g": gain((L, D_MODEL)),
        "ln1_b": nrm((L, D_MODEL), 0.01),
        "w_router": nrm((L, D_MODEL, N_EXPERTS), D_MODEL ** -0.5),
        "router_bias": nrm((L, N_EXPERTS), 0.01),
        "w_e_gate": nrm((L, N_EXPERTS, D_MODEL, D_EXPERT), D_MODEL ** -0.5),
        "w_e_up": nrm((L, N_EXPERTS, D_MODEL, D_EXPERT), D_MODEL ** -0.5),
        "w_e_down": nrm((L, N_EXPERTS, D_EXPERT, D_MODEL), BETA * D_EXPERT ** -0.5),
        "w_sh_gate": nrm((L, D_MODEL, D_SHARED), D_MODEL ** -0.5),
        "w_sh_up": nrm((L, D_MODEL, D_SHARED), D_MODEL ** -0.5),
        "w_sh_down": nrm((L, D_SHARED, D_MODEL), BETA * D_SHARED ** -0.5),
        "ln2_g": gain((L, D_MODEL)),
        "ln2_b": nrm((L, D_MODEL), 0.01),
    }


def reference(x_prompt, x_sample, c_prompt, c_sample, w_mod, b_mod, w_in, q_norm_g, w_uq, kv_norm_g, w_ukv,
              sgu_norm_g, sgu_norm_b, w_spatial, b_spatial, conv_w, conv_b, w_rgate, b_rgate, w_igate, b_igate,
              lru_logit, w_branch, w_out, ln1_g, ln1_b, w_router, router_bias, w_e_gate, w_e_up, w_e_down,
              w_sh_gate, w_sh_up, w_sh_down, ln2_g, ln2_b):
    weights = (w_mod, b_mod, w_in, q_norm_g, w_uq, kv_norm_g, w_ukv, sgu_norm_g, sgu_norm_b, w_spatial, b_spatial,
               conv_w, conv_b, w_rgate, b_rgate, w_igate, b_igate, lru_logit, w_branch, w_out, ln1_g, ln1_b,
               w_router, router_bias, w_e_gate, w_e_up, w_e_down, w_sh_gate, w_sh_up, w_sh_down, ln2_g, ln2_b)
    y_prompt = encoder_trunk(x_prompt, c_prompt, weights)
    y_sample = encoder_trunk(x_sample, c_sample, weights)
    return (y_prompt, y_sample)
```

```python
import functools

import numpy as np
import jax
import jax.numpy as jnp
from jax import lax
from jax.experimental import pallas as pl
from jax.experimental.pallas import tpu as pltpu

F32 = jnp.float32
BF16 = jnp.bfloat16
U32 = jnp.uint32
I32 = jnp.int32

V7X_LANES = 128
V7X_SUBLANES = 8
V7X_VMEM_BYTES = 64 * 1024 * 1024

LN_EPS = 1e-5
RMS_EPS = 1e-6
ROPE_THETA = 10000.0
ROPE_DIM = 64
NOPE_DIM = 128
V_DIM = 128
HEAD_PAD = 256
CHUNK = 128
LRU_C = 8.0
TOP_K = 8
N_GROUPS = 8
TOPK_GROUPS = 4
ROUTED_SCALE = 2.5
EXPERT_ROWS = 256
HALO_ROWS = 16


def _params(sem, vmem_bytes):
    assert vmem_bytes <= V7X_VMEM_BYTES - 4 * 1024 * 1024
    return pltpu.CompilerParams(dimension_semantics=sem, vmem_limit_bytes=int(vmem_bytes))


class _Geom:
    def __init__(self, b1, s1, b2, s2):
        assert s1 % s2 == 0
        self.b1, self.s1, self.b2, self.s2 = b1, s1, b2, s2
        self.unit = s2
        self.r = s1 // s2
        self.n1 = b1 * self.r
        self.T = b1 * s1 + b2 * s2
        self.nrows = b1 + b2

    def unit_of_tile(self, i, tm):
        return i // (self.unit // tm)

    def row_of_tile(self, i, tm):
        u = self.unit_of_tile(i, tm)
        return jnp.where(u < self.n1, u // self.r, self.b1 + u - self.n1)

    def pos0_of_tile(self, i, tm):
        u = self.unit_of_tile(i, tm)
        return jnp.where(u < self.n1, (i * tm) % self.s1, (i * tm) % self.s2)

    def seqlen_of_tile(self, i, tm):
        u = self.unit_of_tile(i, tm)
        return jnp.where(u < self.n1, self.s1, self.s2)


def _mod_spec(geom, tm, layer, chunk, d):
    def imap(i, *_):
        return (layer * 8 * 6 + geom.row_of_tile(i, tm) * 6 + chunk, 0, 0)
    return pl.BlockSpec((None, 1, d), imap)


def _ln(x):
    mu = jnp.mean(x, axis=-1, keepdims=True)
    xc = x - mu
    var = jnp.mean(xc * xc, axis=-1, keepdims=True)
    return xc * lax.rsqrt(var + LN_EPS)


def _rms(x, g):
    return x * lax.rsqrt(jnp.mean(x * x, axis=-1, keepdims=True) + RMS_EPS) * g


def _gelu(x):
    return 0.5 * x * (1.0 + lax.erf(x * np.float32(1.0 / np.sqrt(2.0))))


def _sigmoid(x):
    return 1.0 / (1.0 + jnp.exp(-x))


def _dot(a, b):
    return jnp.dot(a, b, preferred_element_type=F32)


def _dot_nt(a, b):
    return lax.dot_general(a, b, (((1,), (1,)), ((), ())), preferred_element_type=F32)


def _split(a):
    hi = a.astype(BF16)
    lo = (a - hi.astype(F32)).astype(BF16)
    return hi, lo


def _dot3(a, b):
    ah, al = _split(a)
    bh, bl = _split(b)
    return _dot(ah, bh) + (_dot(ah, bl) + _dot(al, bh))


def _dot3_nt(a, b):
    ah, al = _split(a)
    bh, bl = _split(b)
    return _dot_nt(ah, bh) + (_dot_nt(ah, bl) + _dot_nt(al, bh))


def _mod_kernel(c_ref, w_ref, b_ref, o_ref):
    c = c_ref[...]
    s = c * _sigmoid(c)
    o_ref[...] = _dot3(s, w_ref[...]) + b_ref[...]


def _mod_all(c8, w_mod, b_mod):
    L, d, n = w_mod.shape
    tn = min(n, 1536)
    assert n % tn == 0
    return pl.pallas_call(
        _mod_kernel,
        out_shape=jax.ShapeDtypeStruct((L, 8, n), F32),
        grid=(L, n // tn),
        in_specs=[pl.BlockSpec((8, d), lambda l, j: (0, 0)),
                  pl.BlockSpec((None, d, tn), lambda l, j: (l, 0, j)),
                  pl.BlockSpec((None, 1, tn), lambda l, j: (l, 0, j))],
        out_specs=pl.BlockSpec((None, 8, tn), lambda l, j: (l, 0, j)),
        compiler_params=_params(("arbitrary", "arbitrary"), 48 * 2**20),
        name="mod",
    )(c8, w_mod, b_mod.reshape(L, 1, n))


def _lnmod_kernel(x_ref, sc_ref, sh_ref, o_ref):
    o_ref[...] = (_ln(x_ref[...]) * (1.0 + sc_ref[...]) + sh_ref[...]).astype(o_ref.dtype)


def _lnmod(x, modr, geom, layer):
    T, d = x.shape
    tm = min(512, geom.unit)
    return pl.pallas_call(
        _lnmod_kernel,
        out_shape=jax.ShapeDtypeStruct((T, d), BF16),
        grid=(T // tm,),
        in_specs=[pl.BlockSpec((tm, d), lambda i: (i, 0)),
                  _mod_spec(geom, tm, layer, 1, d),
                  _mod_spec(geom, tm, layer, 0, d)],
        out_specs=pl.BlockSpec((tm, d), lambda i: (i, 0)),
        compiler_params=_params(("arbitrary",), 32 * 2**20),
        name="lnmod",
    )(x, modr, modr)


def _mm_kernel(a_ref, b_ref, o_ref):
    o_ref[...] = _dot(a_ref[...], b_ref[...]).astype(o_ref.dtype)


def _matmul(a, b, tm, tn, out_dtype=BF16, name="mm"):
    M, K = a.shape
    _, N = b.shape
    tm, tn = min(tm, M), min(tn, N)
    assert M % tm == 0 and N % tn == 0
    return pl.pallas_call(
        _mm_kernel,
        out_shape=jax.ShapeDtypeStruct((M, N), out_dtype),
        grid=(M // tm, N // tn),
        in_specs=[pl.BlockSpec((tm, K), lambda i, j: (i, 0)),
                  pl.BlockSpec((K, tn), lambda i, j: (0, j))],
        out_specs=pl.BlockSpec((tm, tn), lambda i, j: (i, j)),
        compiler_params=_params(("arbitrary", "arbitrary"), 48 * 2**20),
        name=name,
    )(a, b)


def _q_kernel(h_ref, win_ref, g_ref, w_ref, cc_ref, ss_ref, o_ref, *, heads, scale):
    ql = _dot(h_ref[...], win_ref[...])
    qn = _rms(ql, g_ref[...]).astype(BF16)
    y = _dot(qn, w_ref[...])
    cc = cc_ref[...] * scale
    ss = ss_ref[...] * scale
    for h in range(heads):
        base = h * HEAD_PAD
        nope = y[:, base:base + NOPE_DIM]
        r = y[:, base + NOPE_DIM:base + HEAD_PAD]
        rr = r * cc + pltpu.roll(r, 64, 1) * ss
        o_ref[:, base:base + NOPE_DIM] = (nope * scale).astype(o_ref.dtype)
        o_ref[:, base + NOPE_DIM:base + HEAD_PAD] = rr.astype(o_ref.dtype)


def _q_path(h, w_qin, q_norm_g, wq_ext, cc, ss, geom, heads):
    T, d = h.shape
    ql = w_qin.shape[1]
    tm = min(512, geom.unit)
    scale = float((NOPE_DIM + ROPE_DIM) ** -0.5)
    pos_spec = pl.BlockSpec((tm, V7X_LANES), lambda i: (geom.pos0_of_tile(i, tm) // tm, 0))
    return pl.pallas_call(
        functools.partial(_q_kernel, heads=heads, scale=scale),
        out_shape=jax.ShapeDtypeStruct((T, heads * HEAD_PAD), BF16),
        grid=(T // tm,),
        in_specs=[pl.BlockSpec((tm, d), lambda i: (i, 0)),
                  pl.BlockSpec((d, ql), lambda i: (0, 0)),
                  pl.BlockSpec((1, ql), lambda i: (0, 0)),
                  pl.BlockSpec((ql, heads * HEAD_PAD), lambda i: (0, 0)),
                  pos_spec, pos_spec],
        out_specs=pl.BlockSpec((tm, heads * HEAD_PAD), lambda i: (i, 0)),
        compiler_params=_params(("arbitrary",), 48 * 2**20),
        name="q_path",
    )(h, w_qin, q_norm_g.reshape(1, ql), wq_ext, cc, ss)


def _kv_kernel(h_ref, win_ref, g_ref, wk_ref, wv_ref, cc_ref, ss_ref, k_ref, v_ref, *, heads, kvl):
    y = _dot(h_ref[...], win_ref[...])
    c = y[:, :kvl]
    r = y[:, kvl:kvl + V7X_LANES]
    cn = _rms(c, g_ref[...]).astype(BF16)
    kn = _dot(cn, wk_ref[...])
    v_ref[...] = _dot(cn, wv_ref[...]).astype(v_ref.dtype)
    rr = (r * cc_ref[...] + pltpu.roll(r, 64, 1) * ss_ref[...]).astype(k_ref.dtype)
    for h in range(heads):
        base = h * HEAD_PAD
        k_ref[:, base:base + NOPE_DIM] = kn[:, h * NOPE_DIM:(h + 1) * NOPE_DIM].astype(k_ref.dtype)
        k_ref[:, base + NOPE_DIM:base + HEAD_PAD] = rr


def _kv_path(h, w_kvin, kv_norm_g, wk, wv, cc, ss, geom, heads):
    T, d = h.shape
    kvl = wk.shape[0]
    nin = w_kvin.shape[1]
    tm = min(512, geom.unit)
    pos_spec = pl.BlockSpec((tm, V7X_LANES), lambda i: (geom.pos0_of_tile(i, tm) // tm, 0))
    return pl.pallas_call(
        functools.partial(_kv_kernel, heads=heads, kvl=kvl),
        out_shape=(jax.ShapeDtypeStruct((T, heads * HEAD_PAD), BF16),
                   jax.ShapeDtypeStruct((T, heads * V_DIM), BF16)),
        grid=(T // tm,),
        in_specs=[pl.BlockSpec((tm, d), lambda i: (i, 0)),
                  pl.BlockSpec((d, nin), lambda i: (0, 0)),
                  pl.BlockSpec((1, kvl), lambda i: (0, 0)),
                  pl.BlockSpec((kvl, heads * NOPE_DIM), lambda i: (0, 0)),
                  pl.BlockSpec((kvl, heads * V_DIM), lambda i: (0, 0)),
                  pos_spec, pos_spec],
        out_specs=(pl.BlockSpec((tm, heads * HEAD_PAD), lambda i: (i, 0)),
                   pl.BlockSpec((tm, heads * V_DIM), lambda i: (i, 0))),
        compiler_params=_params(("arbitrary",), 48 * 2**20),
        name="kv_path",
    )(h, w_kvin, kv_norm_g.reshape(1, kvl), wk, wv, cc, ss)


def _attn_kernel(qblk_ref, head_ref, kunit_ref, nparts_ref, q_ref, *refs, parts, tkc):
    k_refs = refs[:parts]
    v_refs = refs[parts:2 * parts]
    o_ref = refs[2 * parts]
    w = pl.program_id(0)
    nparts = nparts_ref[w]
    q = q_ref[...]
    tq = q.shape[0]
    unit = k_refs[0].shape[0]
    nkc = unit // tkc

    def make_body(k_ref, v_ref):
        def body(j, carry):
            m, l, acc = carry
            off = pl.multiple_of(j * tkc, tkc)
            kj = k_ref[pl.ds(off, tkc), :]
            vj = v_ref[pl.ds(off, tkc), :]
            s = _dot_nt(q, kj)
            m_new = jnp.maximum(m, jnp.max(s, axis=-1, keepdims=True))
            alpha = jnp.exp(m - m_new)
            p = jnp.exp(s - m_new)
            l = alpha * l + jnp.sum(p, axis=-1, keepdims=True)
            acc = alpha * acc + _dot(p.astype(BF16), vj)
            return m_new, l, acc
        return body

    carry = (jnp.full((tq, 1), -jnp.inf, F32), jnp.zeros((tq, 1), F32), jnp.zeros((tq, V_DIM), F32))
    for p in range(parts):
        n = jnp.where(p < nparts, nkc, 0)
        carry = lax.fori_loop(0, n, make_body(k_refs[p], v_refs[p]), carry)
    _, l, acc = carry
    o_ref[...] = (acc / l).astype(o_ref.dtype)


def _attention(q, k, v, geom, heads):
    T = q.shape[0]
    unit = geom.unit
    parts = geom.r
    tq = min(512, unit)
    tkc = min(512, unit)
    qblk, head, kunit, nparts = [], [], [], []
    for b in range(geom.b1 + geom.b2):
        if b < geom.b1:
            base, slen = b * geom.s1, geom.s1
        else:
            base, slen = geom.b1 * geom.s1 + (b - geom.b1) * geom.s2, geom.s2
        for h in range(heads):
            for i in range(slen // tq):
                qblk.append(base // tq + i)
                head.append(h)
                kunit.append(base // unit)
                nparts.append(slen // unit)
    tabs = [jnp.asarray(np.asarray(t, np.int32)) for t in (qblk, head, kunit, nparts)]
    nwork = len(qblk)

    def q_map(w, qb, hd, ku, npt):
        return (qb[w], hd[w])

    def kv_map(p):
        def imap(w, qb, hd, ku, npt):
            return (ku[w] + jnp.minimum(p, npt[w] - 1), hd[w])
        return imap

    grid_spec = pltpu.PrefetchScalarGridSpec(
        num_scalar_prefetch=4,
        grid=(nwork,),
        in_specs=([pl.BlockSpec((tq, HEAD_PAD), q_map)]
                  + [pl.BlockSpec((unit, HEAD_PAD), kv_map(p)) for p in range(parts)]
                  + [pl.BlockSpec((unit, V_DIM), kv_map(p)) for p in range(parts)]),
        out_specs=pl.BlockSpec((tq, V_DIM), q_map),
    )
    return pl.pallas_call(
        functools.partial(_attn_kernel, parts=parts, tkc=tkc),
        out_shape=jax.ShapeDtypeStruct((T, heads * V_DIM), BF16),
        grid_spec=grid_spec,
        compiler_params=_params(("arbitrary",), 48 * 2**20),
        name="attention",
    )(*tabs, q, *([k] * parts), *([v] * parts))


def _sgu_kernel(u_ref, v_ref, g_ref, b_ref, ws_ref, bs_ref, o_ref, *, groups):
    tm = u_ref.shape[0]
    v = _gelu(v_ref[...].astype(F32))
    v = (_ln(v) * g_ref[...] + b_ref[...]).astype(BF16)
    for c in range(tm // CHUNK):
        rows = slice(c * CHUNK, (c + 1) * CHUNK)
        for g in range(groups):
            cols = slice(g * V7X_LANES, (g + 1) * V7X_LANES)
            mixed = _dot(ws_ref[g], v[rows, cols]) + bs_ref[:, cols]
            u = _gelu(u_ref[rows, cols].astype(F32))
            o_ref[rows, cols] = (u * mixed).astype(o_ref.dtype)


def _sgu(proj, sgu_norm_g, sgu_norm_b, ws, bs_full, geom, width):
    T = proj.shape[0]
    groups = width // V7X_LANES
    tm = min(256, geom.unit)
    return pl.pallas_call(
        functools.partial(_sgu_kernel, groups=groups),
        out_shape=jax.ShapeDtypeStruct((T, width), BF16),
        grid=(T // tm,),
        in_specs=[pl.BlockSpec((tm, width), lambda i: (i, 0)),
                  pl.BlockSpec((tm, width), lambda i: (i, 1)),
                  pl.BlockSpec((1, width), lambda i: (0, 0)),
                  pl.BlockSpec((1, width), lambda i: (0, 0)),
                  pl.BlockSpec((groups, CHUNK, CHUNK), lambda i: (0, 0, 0)),
                  pl.BlockSpec((CHUNK, width), lambda i: (0, 0))],
        out_specs=pl.BlockSpec((tm, width), lambda i: (i, 0)),
        compiler_params=_params(("arbitrary",), 32 * 2**20),
        name="sgu",
    )(proj, proj, sgu_norm_g.reshape(1, width), sgu_norm_b.reshape(1, width), ws, bs_full)


def _scan_tile(a, u, reverse):
    tm = a.shape[0]
    row = lax.broadcasted_iota(I32, (tm, 1), 0)
    s = 1
    while s < tm:
        shift = (tm - s) if reverse else s
        valid = (row < tm - s) if reverse else (row >= s)
        a_sh = jnp.where(valid, pltpu.roll(a, shift, 0), 1.0)
        u_sh = jnp.where(valid, pltpu.roll(u, shift, 0), 0.0)
        u = a * u_sh + u
        a = a * a_sh
        s *= 2
    return a, u


def _lru_kernel(*refs, reverse, blocks, geom, tm, ntiles):
    if reverse:
        (x_ref, prev_ref, next_ref, cw_ref, cb_ref, wg_ref, br_ref, bi_ref, lg_ref,
         hf_ref, rg_ref, o_ref, carry_ref) = refs
    else:
        (x_ref, prev_ref, next_ref, cw_ref, cb_ref, wg_ref, br_ref, bi_ref, lg_ref,
         o_ref, carry_ref) = refs
    step = pl.program_id(0)
    i = (ntiles - 1 - step) if reverse else step
    pos0 = geom.pos0_of_tile(i, tm)
    at_start = pos0 == 0
    at_end = pos0 + tm == geom.seqlen_of_tile(i, tm)
    x = x_ref[...].astype(F32)
    row = lax.broadcasted_iota(I32, (tm, 1), 0)
    keep_prev = jnp.where(at_start, 0.0, 1.0)
    keep_next = jnp.where(at_end, 0.0, 1.0)
    prev = prev_ref[...].astype(F32)
    p6 = prev[HALO_ROWS - 2:HALO_ROWS - 1, :] * keep_prev
    p7 = prev[HALO_ROWS - 1:HALO_ROWS, :] * keep_prev
    n0 = next_ref[...].astype(F32)[0:1, :] * keep_next
    xm1 = jnp.where(row == 0, p7, pltpu.roll(x, 1, 0))
    xm2 = jnp.where(row == 0, p6, jnp.where(row == 1, p7, pltpu.roll(x, 2, 0)))
    xp1 = jnp.where(row == tm - 1, n0, pltpu.roll(x, tm - 1, 0))
    xc = cb_ref[...] + xm2 * cw_ref[0:1, :] + xm1 * cw_ref[1:2, :] + x * cw_ref[2:3, :] + xp1 * cw_ref[3:4, :]
    xcb = xc.astype(BF16)
    rs, is_ = [], []
    for n in range(blocks):
        cols = slice(n * V7X_LANES, (n + 1) * V7X_LANES)
        g = _dot(xcb[:, cols], wg_ref[n])
        rs.append(g[:, :V7X_LANES])
        is_.append(g[:, V7X_LANES:])
    r = _sigmoid(jnp.concatenate(rs, axis=1) + br_ref[...])
    ig = _sigmoid(jnp.concatenate(is_, axis=1) + bi_ref[...])
    lg = lg_ref[...]
    softplus = jnp.maximum(-lg, 0.0) + jnp.log(1.0 + jnp.exp(-jnp.abs(lg)))
    log_a = -LRU_C * r * softplus
    a = jnp.exp(log_a)
    mult = jnp.sqrt(1.0 - a * a)
    if reverse:
        mult = jnp.where(jnp.logical_and(at_end, row == tm - 1), 1.0, mult)
    else:
        mult = jnp.where(jnp.logical_and(at_start, row == 0), 1.0, mult)
    u = mult * (ig * xc)
    acum, hloc = _scan_tile(a, u, reverse)
    fresh = at_end if reverse else at_start
    carry = jnp.where(fresh, 0.0, carry_ref[0:1, :])
    h = hloc + acum * carry
    if reverse:
        carry_ref[0:1, :] = h[0:1, :]
        o_ref[...] = ((hf_ref[...].astype(F32) + h) * _gelu(rg_ref[...].astype(F32))).astype(o_ref.dtype)
    else:
        carry_ref[0:1, :] = h[tm - 1:tm, :]
        o_ref[...] = h.astype(o_ref.dtype)


def _lru_pass(proj, xcol, gcol, conv_w, conv_b, wg, br, bi, lg, geom, width, reverse, h_fwd=None):
    T = proj.shape[0]
    tm = min(256, geom.unit)
    ntiles = T // tm
    blocks = width // V7X_LANES
    nb8 = T // HALO_ROWS
    t8 = tm // HALO_ROWS

    def tile(step):
        return (ntiles - 1 - step) if reverse else step

    in_specs = [pl.BlockSpec((tm, width), lambda s: (tile(s), xcol)),
                pl.BlockSpec((HALO_ROWS, width), lambda s: (jnp.maximum(tile(s) * t8 - 1, 0), xcol)),
                pl.BlockSpec((HALO_ROWS, width), lambda s: (jnp.minimum((tile(s) + 1) * t8, nb8 - 1), xcol)),
                pl.BlockSpec((4, width), lambda s: (0, 0)),
                pl.BlockSpec((1, width), lambda s: (0, 0)),
                pl.BlockSpec((blocks, V7X_LANES, 2 * V7X_LANES), lambda s: (0, 0, 0)),
                pl.BlockSpec((1, width), lambda s: (0, 0)),
                pl.BlockSpec((1, width), lambda s: (0, 0)),
                pl.BlockSpec((1, width), lambda s: (0, 0))]
    args = [proj, proj, proj, conv_w, conv_b.reshape(1, width), wg, br.reshape(1, width),
            bi.reshape(1, width), lg.reshape(1, width)]
    if reverse:
        in_specs += [pl.BlockSpec((tm, width), lambda s: (tile(s), 0)),
                     pl.BlockSpec((tm, width), lambda s: (tile(s), gcol))]
        args += [h_fwd, proj]
    return pl.pallas_call(
        functools.partial(_lru_kernel, reverse=reverse, blocks=blocks, geom=geom, tm=tm, ntiles=ntiles),
        out_shape=jax.ShapeDtypeStruct((T, width), BF16),
        grid=(ntiles,),
        in_specs=in_specs,
        out_specs=pl.BlockSpec((tm, width), lambda s: (tile(s), 0)),
        scratch_shapes=[pltpu.VMEM((V7X_SUBLANES, width), F32)],
        compiler_params=_params(("arbitrary",), 40 * 2**20),
        name="lru_bwd" if reverse else "lru_fwd",
    )(*args)


def _merge_kernel(oa_ref, ob_ref, oc_ref, gate_ref, w_ref, y_ref, acc_ref):
    n = pl.program_id(1)

    def branch(o_ref, first):
        p = _dot(o_ref[...], w_ref[...]) * _sigmoid(gate_ref[...].astype(F32))
        if first:
            acc_ref[...] = p
        else:
            acc_ref[...] += p

    @pl.when(n == 0)
    def _():
        branch(oa_ref, True)

    @pl.when(n == 1)
    def _():
        branch(ob_ref, False)

    @pl.when(n == 2)
    def _():
        branch(oc_ref, False)

    @pl.when(n == 3)
    def _():
        y_ref[...] = _dot(acc_ref[...].astype(BF16), w_ref[...]).astype(y_ref.dtype)


def _merge(oa, ob, oc, proj, gate_col0, w4, geom):
    T, d = oa.shape
    tm = min(512, geom.unit)
    return pl.pallas_call(
        _merge_kernel,
        out_shape=jax.ShapeDtypeStruct((T, d), BF16),
        grid=(T // tm, 4),
        in_specs=[pl.BlockSpec((tm, d), lambda i, n: (i, 0)),
                  pl.BlockSpec((tm, d), lambda i, n: (i, 0)),
                  pl.BlockSpec((tm, d), lambda i, n: (i, 0)),
                  pl.BlockSpec((tm, d), lambda i, n: (i, gate_col0 + jnp.minimum(n, 2))),
                  pl.BlockSpec((None, d, d), lambda i, n: (n, 0, 0))],
        out_specs=pl.BlockSpec((tm, d), lambda i, n: (i, 0)),
        scratch_shapes=[pltpu.VMEM((tm, d), F32)],
        compiler_params=_params(("arbitrary", "arbitrary"), 52 * 2**20),
        name="merge",
    )(oa, ob, oc, proj, w4)


def _pack_pairs(a):
    w = a.shape[1] // 2
    lo = lax.bitcast_convert_type(a[:, :w].astype(BF16).astype(F32), U32) >> 16
    hi = lax.bitcast_convert_type(a[:, w:].astype(BF16).astype(F32), U32) & np.uint32(0xFFFF0000)
    return lo | hi


def _unpack_pairs(p):
    lo = lax.bitcast_convert_type(p << 16, F32)
    hi = lax.bitcast_convert_type(p & np.uint32(0xFFFF0000), F32)
    return lo, hi


def _res_mixer_kernel(x_ref, y_ref, gate_ref, g_ref, b_ref, sc_ref, sh_ref, wr_ref,
                      x1_ref, h_ref, hp_ref, lg_ref, *, alpha):
    z = alpha * x_ref[...] + gate_ref[...] * y_ref[...].astype(F32)
    x1 = _ln(z) * g_ref[...] + b_ref[...]
    x1_ref[...] = x1
    h = _ln(x1) * (1.0 + sc_ref[...]) + sh_ref[...]
    h_ref[...] = h.astype(h_ref.dtype)
    hp_ref[...] = _pack_pairs(h)
    lg_ref[...] = _dot3_nt(wr_ref[...], h)


def _res_mixer(x, y, modr, ln_g, ln_b, w_router_t, geom, layer, alpha):
    T, d = x.shape
    ne = w_router_t.shape[0]
    tm = min(256, geom.unit)
    return pl.pallas_call(
        functools.partial(_res_mixer_kernel, alpha=alpha),
        out_shape=(jax.ShapeDtypeStruct((T, d), F32),
                   jax.ShapeDtypeStruct((T, d), BF16),
                   jax.ShapeDtypeStruct((T, d // 2), U32),
                   jax.ShapeDtypeStruct((ne, T), F32)),
        grid=(T // tm,),
        in_specs=[pl.BlockSpec((tm, d), lambda i: (i, 0)),
                  pl.BlockSpec((tm, d), lambda i: (i, 0)),
                  _mod_spec(geom, tm, layer, 2, d),
                  pl.BlockSpec((1, d), lambda i: (0, 0)),
                  pl.BlockSpec((1, d), lambda i: (0, 0)),
                  _mod_spec(geom, tm, layer, 4, d),
                  _mod_spec(geom, tm, layer, 3, d),
                  pl.BlockSpec((ne, d), lambda i: (0, 0))],
        out_specs=(pl.BlockSpec((tm, d), lambda i: (i, 0)),
                   pl.BlockSpec((tm, d), lambda i: (i, 0)),
                   pl.BlockSpec((tm, d // 2), lambda i: (i, 0)),
                   pl.BlockSpec((ne, tm), lambda i: (0, i))),
        compiler_params=_params(("arbitrary",), 40 * 2**20),
        name="res_mixer",
    )(x, y, modr, ln_g.reshape(1, d), ln_b.reshape(1, d), modr, modr, w_router_t)


def _res_moe_kernel(*refs, alpha, has_next):
    if has_next:
        x_ref, pair_ref, sh_ref, gate_ref, g_ref, b_ref, sc_ref, shf_ref, x2_ref, h_ref = refs
    else:
        x_ref, pair_ref, sh_ref, gate_ref, g_ref, b_ref, x2_ref = refs
    w = pair_ref.shape[2]
    lo = jnp.zeros((x_ref.shape[0], w), F32)
    hi = jnp.zeros((x_ref.shape[0], w), F32)
    for k in range(pair_ref.shape[0]):
        l, h = _unpack_pairs(pair_ref[k])
        lo = lo + l
        hi = hi + h
    y = jnp.concatenate([lo, hi], axis=1) + sh_ref[...].astype(F32)
    z = alpha * x_ref[...] + gate_ref[...] * y
    x2 = _ln(z) * g_ref[...] + b_ref[...]
    x2_ref[...] = x2
    if has_next:
        h_ref[...] = (_ln(x2) * (1.0 + sc_ref[...]) + shf_ref[...]).astype(h_ref.dtype)


def _res_moe(x, pairbuf, shared, modr, ln_g, ln_b, geom, layer, alpha, has_next):
    T, d = x.shape
    tm = min(256, geom.unit)
    in_specs = [pl.BlockSpec((tm, d), lambda i: (i, 0)),
                pl.BlockSpec((TOP_K, tm, d // 2), lambda i: (0, i, 0)),
                pl.BlockSpec((tm, d), lambda i: (i, 0)),
                _mod_spec(geom, tm, layer, 5, d),
                pl.BlockSpec((1, d), lambda i: (0, 0)),
                pl.BlockSpec((1, d), lambda i: (0, 0))]
    args = [x, pairbuf, shared, modr, ln_g.reshape(1, d), ln_b.reshape(1, d)]
    out_shape = [jax.ShapeDtypeStruct((T, d), F32)]
    out_specs = [pl.BlockSpec((tm, d), lambda i: (i, 0))]
    if has_next:
        in_specs += [_mod_spec(geom, tm, layer + 1, 1, d), _mod_spec(geom, tm, layer + 1, 0, d)]
        args += [modr, modr]
        out_shape.append(jax.ShapeDtypeStruct((T, d), BF16))
        out_specs.append(pl.BlockSpec((tm, d), lambda i: (i, 0)))
    return pl.pallas_call(
        functools.partial(_res_moe_kernel, alpha=alpha, has_next=has_next),
        out_shape=tuple(out_shape),
        grid=(T // tm,),
        in_specs=in_specs,
        out_specs=tuple(out_specs),
        compiler_params=_params(("arbitrary",), 40 * 2**20),
        name="res_moe",
    )(*args)


def _argmax_rows(vals, row, nrow):
    m = jnp.max(vals, axis=0, keepdims=True)
    idx = jnp.min(jnp.where(vals == m, row, nrow), axis=0, keepdims=True)
    return m, idx


def _route_kernel(lg_ref, bias_ref, idx_ref, w_ref):
    ne, tr = lg_ref.shape
    gsz = ne // N_GROUPS
    scores = _sigmoid(lg_ref[...])
    biased = scores + bias_ref[...]
    neg = -jnp.inf
    grow = lax.broadcasted_iota(I32, (gsz, tr), 0)
    gscore = []
    for g in range(N_GROUPS):
        blk = biased[g * gsz:(g + 1) * gsz, :]
        m1, i1 = _argmax_rows(blk, grow, gsz)
        m2 = jnp.max(jnp.where(grow == i1, neg, blk), axis=0, keepdims=True)
        gscore.append(m1 + m2)
    gs = jnp.concatenate(gscore, axis=0)
    gidx = lax.broadcasted_iota(I32, (N_GROUPS, tr), 0)
    keep = jnp.zeros((N_GROUPS, tr), F32)
    for _ in range(TOPK_GROUPS):
        _, gi = _argmax_rows(gs, gidx, N_GROUPS)
        sel = gidx == gi
        keep = jnp.where(sel, 1.0, keep)
        gs = jnp.where(sel, neg, gs)
    keep_full = jnp.concatenate(
        [jnp.broadcast_to(keep[g:g + 1, :], (gsz, tr)) for g in range(N_GROUPS)], axis=0)
    masked = jnp.where(keep_full > 0.5, biased, neg)
    row = lax.broadcasted_iota(I32, (ne, tr), 0)
    idxs, ws = [], []
    for _ in range(TOP_K):
        _, ei = _argmax_rows(masked, row, ne)
        sel = row == ei
        ws.append(jnp.sum(jnp.where(sel, scores, 0.0), axis=0, keepdims=True))
        idxs.append(ei)
        masked = jnp.where(sel, neg, masked)
    wk = jnp.concatenate(ws, axis=0)
    idx_ref[...] = jnp.concatenate(idxs, axis=0)
    w_ref[...] = wk / jnp.sum(wk, axis=0, keepdims=True) * ROUTED_SCALE


def _route(logits_t, router_bias):
    ne, T = logits_t.shape
    tr = min(512, T)
    return pl.pallas_call(
        _route_kernel,
        out_shape=(jax.ShapeDtypeStruct((TOP_K, T), I32), jax.ShapeDtypeStruct((TOP_K, T), F32)),
        grid=(T // tr,),
        in_specs=[pl.BlockSpec((ne, tr), lambda i: (0, i)),
                  pl.BlockSpec((ne, 1), lambda i: (0, 0))],
        out_specs=(pl.BlockSpec((TOP_K, tr), lambda i: (0, i)),
                   pl.BlockSpec((TOP_K, tr), lambda i: (0, i))),
        compiler_params=_params(("arbitrary",), 32 * 2**20),
        name="route",
    )(logits_t, router_bias.reshape(ne, 1))


def _expert_kernel(be_ref, nv_ref, nact_ref,
                   tok_ref, tokn_ref, dst_ref, w_ref, x_hbm, wg_ref, wu_ref, wd_ref,
                   out_hbm,
                   xbuf, ybuf, wgb, wub, wdb, gsem, ssem):
    b = pl.program_id(0)
    nact = nact_ref[0]
    slot = b % 2
    rows = xbuf.shape[1]

    def gather_copy(tok, r, sl):
        return pltpu.make_async_copy(x_hbm.at[pl.ds(tok, 1)], xbuf.at[sl, pl.ds(r, 1)], gsem.at[sl])

    def scatter_copy(dst, r, sl):
        return pltpu.make_async_copy(ybuf.at[sl, pl.ds(r, 1)], out_hbm.at[pl.ds(dst, 1)], ssem.at[sl])

    def gather_start(idx_ref, n, sl):
        def body(r, c):
            gather_copy(idx_ref[0, r], r, sl).start()
            return c
        lax.fori_loop(0, n, body, 0)

    def gather_wait(n, sl):
        def body(r, c):
            gather_copy(0, r, sl).wait()
            return c
        lax.fori_loop(0, n, body, 0)

    def scatter_start(n, sl):
        def body(r, c):
            scatter_copy(dst_ref[0, r], r, sl).start()
            return c
        lax.fori_loop(0, n, body, 0)

    def scatter_wait(n, sl):
        def body(r, c):
            scatter_copy(0, r, sl).wait()
            return c
        lax.fori_loop(0, n, body, 0)

    @pl.when(b == 0)
    def _():
        xbuf[...] = jnp.zeros(xbuf.shape, xbuf.dtype)
        gather_start(tok_ref, nv_ref[0], 0)

    @pl.when(b + 1 < nact)
    def _():
        gather_start(tokn_ref, nv_ref[b + 1], 1 - slot)

    @pl.when(b < nact)
    def _():
        prev_e = be_ref[jnp.maximum(b - 1, 0)]

        @pl.when(jnp.logical_or(b == 0, be_ref[b] != prev_e))
        def _():
            wgb[...] = wg_ref[...].astype(BF16)
            wub[...] = wu_ref[...].astype(BF16)
            wdb[...] = wd_ref[...].astype(BF16)

        gather_wait(nv_ref[b], slot)
        lo, hi = _unpack_pairs(xbuf[slot])
        x = jnp.concatenate([lo, hi], axis=1).astype(BF16)
        g = _dot(x, wgb[...])
        u = _dot(x, wub[...])
        wrow = w_ref[...]
        eye = lax.broadcasted_iota(I32, (rows, rows), 0) == lax.broadcasted_iota(I32, (rows, rows), 1)
        wcol = jnp.sum(jnp.where(eye, wrow, 0.0), axis=1, keepdims=True)
        hmid = (g * _sigmoid(g)) * u
        y = _dot(hmid.astype(BF16), wdb[...]) * wcol

        @pl.when(b >= 2)
        def _():
            scatter_wait(nv_ref[jnp.maximum(b - 2, 0)], slot)

        ybuf[slot] = _pack_pairs(y)
        scatter_start(nv_ref[b], slot)

        @pl.when(b == nact - 1)
        def _():
            scatter_wait(nv_ref[b], slot)

            @pl.when(b >= 1)
            def _():
                scatter_wait(nv_ref[jnp.maximum(b - 1, 0)], 1 - slot)


def _experts(hp, block_expert, nvalid, nact, slot_tok, slot_dst, slot_w, w_e_gate, w_e_up, w_e_down, layer):
    T, dh = hp.shape
    d = dh * 2
    nblk = block_expert.shape[0]
    rows = EXPERT_ROWS
    de = w_e_gate.shape[-1]
    tok3 = slot_tok.reshape(nblk, 1, rows)
    dst3 = slot_dst.reshape(nblk, 1, rows)
    w3 = slot_w.reshape(nblk, 1, rows)

    def emap(b, be, nv, na):
        return (layer, be[jnp.minimum(b, jnp.maximum(na[0] - 1, 0))], 0, 0)

    grid_spec = pltpu.PrefetchScalarGridSpec(
        num_scalar_prefetch=3,
        grid=(nblk,),
        in_specs=[pl.BlockSpec((None, 1, rows), lambda b, be, nv, na: (b, 0, 0), memory_space=pltpu.SMEM),
                  pl.BlockSpec((None, 1, rows), lambda b, be, nv, na: (jnp.minimum(b + 1, nblk - 1), 0, 0),
                               memory_space=pltpu.SMEM),
                  pl.BlockSpec((None, 1, rows), lambda b, be, nv, na: (b, 0, 0), memory_space=pltpu.SMEM),
                  pl.BlockSpec((None, 1, rows), lambda b, be, nv, na: (b, 0, 0)),
                  pl.BlockSpec(memory_space=pl.ANY),
                  pl.BlockSpec((None, None, d, de), emap),
                  pl.BlockSpec((None, None, d, de), emap),
                  pl.BlockSpec((None, None, de, d), emap)],
        out_specs=pl.BlockSpec(memory_space=pl.ANY),
        scratch_shapes=[pltpu.VMEM((2, rows, dh), U32),
                        pltpu.VMEM((2, rows, dh), U32),
                        pltpu.VMEM((d, de), BF16),
                        pltpu.VMEM((d, de), BF16),
                        pltpu.VMEM((de, d), BF16),
                        pltpu.SemaphoreType.DMA((2,)),
                        pltpu.SemaphoreType.DMA((2,))],
    )
    out = pl.pallas_call(
        _expert_kernel,
        out_shape=jax.ShapeDtypeStruct((TOP_K * T, dh), U32),
        grid_spec=grid_spec,
        compiler_params=_params(("arbitrary",), 52 * 2**20),
        name="experts",
    )(block_expert, nvalid, nact, tok3, tok3, dst3, w3, hp, w_e_gate, w_e_up, w_e_down)
    return out.reshape(TOP_K, T, dh)


def _dispatch_plan(top_idx_t, top_w_t, n_experts):
    k, T = top_idx_t.shape
    rows = EXPERT_ROWS
    n_pairs = k * T
    pair_expert = top_idx_t.reshape(n_pairs)
    pair_w = top_w_t.reshape(n_pairs)
    order = jnp.argsort(pair_expert).astype(I32)
    bounds = jnp.searchsorted(pair_expert[order], jnp.arange(n_experts + 1, dtype=I32), side='left').astype(I32)
    start = bounds[:-1]
    counts = bounds[1:] - start
    padded = (counts + rows - 1) // rows * rows
    padded_end = jnp.cumsum(padded)
    padded_start = padded_end - padded
    nblk = n_pairs // rows + n_experts
    block_start = jnp.arange(nblk, dtype=I32) * rows
    block_expert = jnp.minimum(jnp.searchsorted(padded_end, block_start, side='right'), n_experts - 1).astype(I32)
    nact = (padded_end[-1] // rows).astype(I32).reshape(1)
    off = block_start - padded_start[block_expert]
    nvalid = jnp.clip(counts[block_expert] - off, 0, rows).astype(I32)
    j = off[:, None] + jnp.arange(rows, dtype=I32)[None, :]
    valid = j < counts[block_expert][:, None]
    src = jnp.clip(start[block_expert][:, None] + j, 0, n_pairs - 1)
    slot_pair = jnp.where(valid, order[src], 0)
    slot_tok = slot_pair % T
    slot_w = jnp.where(valid, pair_w[slot_pair], 0.0)
    return block_expert, nvalid, nact, slot_tok.astype(I32), slot_pair.astype(I32), slot_w.astype(F32)


def _shared_kernel(h_ref, wgu_ref, wd_ref, o_ref):
    gu = _dot(h_ref[...], wgu_ref[...])
    ds = gu.shape[1] // 2
    g, u = gu[:, :ds], gu[:, ds:]
    o_ref[...] = _dot(((g * _sigmoid(g)) * u).astype(BF16), wd_ref[...]).astype(o_ref.dtype)


def _shared(h, wgu, wd, geom):
    T, d = h.shape
    tm = min(1024, geom.unit)
    return pl.pallas_call(
        _shared_kernel,
        out_shape=jax.ShapeDtypeStruct((T, d), BF16),
        grid=(T // tm,),
        in_specs=[pl.BlockSpec((tm, d), lambda i: (i, 0)),
                  pl.BlockSpec(wgu.shape, lambda i: (0, 0)),
                  pl.BlockSpec(wd.shape, lambda i: (0, 0))],
        out_specs=pl.BlockSpec((tm, d), lambda i: (i, 0)),
        compiler_params=_params(("arbitrary",), 40 * 2**20),
        name="shared_expert",
    )(h, wgu, wd)


def _rope_tables(s):
    half = ROPE_DIM // 2
    inv_freq = ROPE_THETA ** (-jnp.arange(half, dtype=F32) / half)
    ang = jnp.arange(s, dtype=F32)[:, None] * inv_freq[None, :]
    cos, sin = jnp.cos(ang), jnp.sin(ang)
    z = jnp.zeros((s, V7X_LANES - ROPE_DIM), F32)
    cc = jnp.concatenate([cos, cos, z], axis=1)
    ss = jnp.concatenate([-sin, sin, z], axis=1)
    return cc, ss


def _swap_halves(w):
    half = w.shape[-1] // 2
    return jnp.concatenate([w[..., half:], w[..., :half]], axis=-1)


def kernel(x_prompt, x_sample, c_prompt, c_sample, w_mod, b_mod, w_in, q_norm_g, w_uq, kv_norm_g, w_ukv, sgu_norm_g, sgu_norm_b, w_spatial, b_spatial, conv_w, conv_b, w_rgate, b_rgate, w_igate, b_igate, lru_logit, w_branch, w_out, ln1_g, ln1_b, w_router, router_bias, w_e_gate, w_e_up, w_e_down, w_sh_gate, w_sh_up, w_sh_down, ln2_g, ln2_b):
    b1, s1, d = x_prompt.shape
    b2, s2, _ = x_sample.shape
    geom = _Geom(b1, s1, b2, s2)
    T = geom.T
    L = w_mod.shape[0]
    alpha = float((2 * L) ** 0.25)
    qlr = q_norm_g.shape[1]
    kvl = kv_norm_g.shape[1]
    heads = w_uq.shape[2] // (NOPE_DIM + ROPE_DIM)
    width = sgu_norm_g.shape[1]
    n_experts = w_router.shape[2]
    assert geom.nrows <= 8 and width == d and heads * V_DIM == d

    x = jnp.concatenate([x_prompt.reshape(b1 * s1, d), x_sample.reshape(b2 * s2, d)], axis=0)
    c8 = jnp.concatenate([c_prompt, c_sample, jnp.zeros((8 - geom.nrows, d), F32)], axis=0)
    modr = _mod_all(c8, w_mod, b_mod).reshape(L * 8 * 6, 1, d)
    cc, ss = _rope_tables(s1)

    o_q, o_kv, o_kr = 0, qlr, qlr + kvl
    o_gm = o_kr + ROPE_DIM
    o_rx = o_gm + 2 * width
    o_rg = o_rx + width
    o_gl = o_rg + width

    h = _lnmod(x, modr, geom, 0)
    for l in range(L):
        wl = w_in[l]
        w_qin = wl[:, o_q:o_kv].astype(BF16)
        w_kr = wl[:, o_kr:o_gm]
        w_kvin = jnp.concatenate([wl[:, o_kv:o_kr], w_kr, _swap_halves(w_kr)], axis=1).astype(BF16)
        w_rest = wl[:, o_gm:].astype(BF16)
        uq = w_uq[l].reshape(qlr, heads, NOPE_DIM + ROPE_DIM)
        wq_ext = jnp.concatenate([uq, _swap_halves(uq[:, :, NOPE_DIM:])], axis=2).reshape(qlr, heads * HEAD_PAD).astype(BF16)
        ukv = w_ukv[l].reshape(kvl, heads, NOPE_DIM + V_DIM)
        wk = ukv[:, :, :NOPE_DIM].reshape(kvl, heads * NOPE_DIM).astype(BF16)
        wv = ukv[:, :, NOPE_DIM:].reshape(kvl, heads * V_DIM).astype(BF16)

        q = _q_path(h, w_qin, q_norm_g[l], wq_ext, cc, ss, geom, heads)
        kk, vv = _kv_path(h, w_kvin, kv_norm_g[l], wk, wv, cc, ss, geom, heads)
        o_a = _attention(q, kk, vv, geom, heads)
        proj = _matmul(h, w_rest, min(1024, geom.unit), width // 2, name="in_proj")
        ws = w_spatial[l].astype(BF16)
        bs_full = jnp.repeat(b_spatial[l].T, V7X_LANES, axis=1)
        o_b = _sgu(proj, sgu_norm_g[l], sgu_norm_b[l], ws, bs_full, geom, width)
        xcol, gcol = 2, 3
        hf = None
        for dr in range(2):
            wg = jnp.concatenate([w_rgate[l, dr], w_igate[l, dr]], axis=2).astype(BF16)
            out = _lru_pass(proj, xcol, gcol, conv_w[l], conv_b[l], wg, b_rgate[l, dr], b_igate[l, dr],
                            lru_logit[l, dr], geom, width, reverse=(dr == 1), h_fwd=hf)
            hf = out
        o_c = hf
        w4 = jnp.concatenate([w_branch[l], w_out[l][None]], axis=0).astype(BF16)
        y = _merge(o_a, o_b, o_c, proj, 4, w4, geom)
        x, h2, hp, logits_t = _res_mixer(x, y, modr, ln1_g[l], ln1_b[l], w_router[l].T, geom, l, alpha)

        top_idx_t, top_w_t = _route(logits_t, router_bias[l])
        plan = _dispatch_plan(top_idx_t, top_w_t, n_experts)
        pairbuf = _experts(hp, *plan, w_e_gate, w_e_up, w_e_down, l)
        wgu = jnp.concatenate([w_sh_gate[l], w_sh_up[l]], axis=1).astype(BF16)
        shared = _shared(h2, wgu, w_sh_down[l].astype(BF16), geom)
        outs = _res_moe(x, pairbuf, shared, modr, ln2_g[l], ln2_b[l], geom, l, alpha, has_next=(l + 1 < L))
        if l + 1 < L:
            x, h = outs
        else:
            (x,) = outs

    y_prompt = x[:b1 * s1].reshape(b1, s1, d)
    y_sample = x[b1 * s1:].reshape(b2, s2, d)
    return (y_prompt, y_sample)
```

```python
import functools

import numpy as np
import jax
import jax.numpy as jnp
from jax import lax
from jax.experimental import pallas as pl
from jax.experimental.pallas import tpu as pltpu

F32 = jnp.float32
BF16 = jnp.bfloat16
U32 = jnp.uint32
I32 = jnp.int32

V7X_LANES = 128
V7X_SUBLANES = 8
V7X_VMEM_BYTES = 64 * 1024 * 1024

LN_EPS = 1e-5
RMS_EPS = 1e-6
ROPE_THETA = 10000.0
ROPE_DIM = 64
NOPE_DIM = 128
V_DIM = 128
HEAD_PAD = 256
CHUNK = 128
LRU_C = 8.0
TOP_K = 8
N_GROUPS = 8
TOPK_GROUPS = 4
ROUTED_SCALE = 2.5
EXPERT_ROWS = 256
HALO_ROWS = 16
DMA_UNROLL = 8


def _params(sem, vmem_bytes):
    assert vmem_bytes <= V7X_VMEM_BYTES - 4 * 1024 * 1024
    return pltpu.CompilerParams(dimension_semantics=sem, vmem_limit_bytes=int(vmem_bytes))


class _Geom:
    def __init__(self, b1, s1, b2, s2):
        assert s1 % s2 == 0
        self.b1, self.s1, self.b2, self.s2 = b1, s1, b2, s2
        self.unit = s2
        self.r = s1 // s2
        self.n1 = b1 * self.r
        self.T = b1 * s1 + b2 * s2
        self.nrows = b1 + b2

    def unit_of_tile(self, i, tm):
        return i // (self.unit // tm)

    def row_of_tile(self, i, tm):
        u = self.unit_of_tile(i, tm)
        return jnp.where(u < self.n1, u // self.r, self.b1 + u - self.n1)

    def pos0_of_tile(self, i, tm):
        u = self.unit_of_tile(i, tm)
        return jnp.where(u < self.n1, (i * tm) % self.s1, (i * tm) % self.s2)

    def seqlen_of_tile(self, i, tm):
        u = self.unit_of_tile(i, tm)
        return jnp.where(u < self.n1, self.s1, self.s2)


def _mod_spec(geom, tm, layer, chunk, d):
    def imap(i, *_):
        return (layer * 8 * 6 + geom.row_of_tile(i, tm) * 6 + chunk, 0, 0)
    return pl.BlockSpec((None, 1, d), imap)


def _ln(x):
    mu = jnp.mean(x, axis=-1, keepdims=True)
    xc = x - mu
    var = jnp.mean(xc * xc, axis=-1, keepdims=True)
    return xc * lax.rsqrt(var + LN_EPS)


def _rms(x, g):
    return x * lax.rsqrt(jnp.mean(x * x, axis=-1, keepdims=True) + RMS_EPS) * g


def _gelu(x):
    return 0.5 * x * (1.0 + lax.erf(x * np.float32(1.0 / np.sqrt(2.0))))


def _sigmoid(x):
    return 1.0 / (1.0 + jnp.exp(-x))


def _dot(a, b):
    return jnp.dot(a, b, preferred_element_type=F32)


def _dot_nt(a, b):
    return lax.dot_general(a, b, (((1,), (1,)), ((), ())), preferred_element_type=F32)


def _split(a):
    hi = a.astype(BF16)
    lo = (a - hi.astype(F32)).astype(BF16)
    return hi, lo


def _dot3(a, b):
    ah, al = _split(a)
    bh, bl = _split(b)
    return _dot(ah, bh) + (_dot(ah, bl) + _dot(al, bh))


def _dot3_nt(a, b):
    ah, al = _split(a)
    bh, bl = _split(b)
    return _dot_nt(ah, bh) + (_dot_nt(ah, bl) + _dot_nt(al, bh))


def _mod_kernel(c_ref, w_ref, b_ref, o_ref):
    c = c_ref[...]
    s = c * _sigmoid(c)
    o_ref[...] = _dot3(s, w_ref[...]) + b_ref[...]


def _mod_all(c8, w_mod, b_mod):
    L, d, n = w_mod.shape
    tn = min(n, 1536)
    assert n % tn == 0
    return pl.pallas_call(
        _mod_kernel,
        out_shape=jax.ShapeDtypeStruct((L, 8, n), F32),
        grid=(L, n // tn),
        in_specs=[pl.BlockSpec((8, d), lambda l, j: (0, 0)),
                  pl.BlockSpec((None, d, tn), lambda l, j: (l, 0, j)),
                  pl.BlockSpec((None, 1, tn), lambda l, j: (l, 0, j))],
        out_specs=pl.BlockSpec((None, 8, tn), lambda l, j: (l, 0, j)),
        compiler_params=_params(("arbitrary", "arbitrary"), 48 * 2**20),
        name="mod",
    )(c8, w_mod, b_mod.reshape(L, 1, n))


def _lnmod_kernel(x_ref, sc_ref, sh_ref, o_ref):
    o_ref[...] = (_ln(x_ref[...]) * (1.0 + sc_ref[...]) + sh_ref[...]).astype(o_ref.dtype)


def _lnmod(x, modr, geom, layer):
    T, d = x.shape
    tm = min(512, geom.unit)
    return pl.pallas_call(
        _lnmod_kernel,
        out_shape=jax.ShapeDtypeStruct((T, d), BF16),
        grid=(T // tm,),
        in_specs=[pl.BlockSpec((tm, d), lambda i: (i, 0)),
                  _mod_spec(geom, tm, layer, 1, d),
                  _mod_spec(geom, tm, layer, 0, d)],
        out_specs=pl.BlockSpec((tm, d), lambda i: (i, 0)),
        compiler_params=_params(("arbitrary",), 32 * 2**20),
        name="lnmod",
    )(x, modr, modr)


def _mm_kernel(a_ref, b_ref, o_ref):
    o_ref[...] = _dot(a_ref[...], b_ref[...]).astype(o_ref.dtype)


def _matmul(a, b, tm, tn, out_dtype=BF16, name="mm"):
    M, K = a.shape
    _, N = b.shape
    tm, tn = min(tm, M), min(tn, N)
    assert M % tm == 0 and N % tn == 0
    return pl.pallas_call(
        _mm_kernel,
        out_shape=jax.ShapeDtypeStruct((M, N), out_dtype),
        grid=(M // tm, N // tn),
        in_specs=[pl.BlockSpec((tm, K), lambda i, j: (i, 0)),
                  pl.BlockSpec((K, tn), lambda i, j: (0, j))],
        out_specs=pl.BlockSpec((tm, tn), lambda i, j: (i, j)),
        compiler_params=_params(("arbitrary", "arbitrary"), 48 * 2**20),
        name=name,
    )(a, b)


def _q_kernel(h_ref, win_ref, g_ref, w_ref, cc_ref, ss_ref, o_ref, *, heads, scale):
    ql = _dot(h_ref[...], win_ref[...])
    qn = _rms(ql, g_ref[...]).astype(BF16)
    y = _dot(qn, w_ref[...])
    cc = cc_ref[...] * scale
    ss = ss_ref[...] * scale
    for h in range(heads):
        base = h * HEAD_PAD
        nope = y[:, base:base + NOPE_DIM]
        r = y[:, base + NOPE_DIM:base + HEAD_PAD]
        rr = r * cc + pltpu.roll(r, 64, 1) * ss
        o_ref[:, base:base + NOPE_DIM] = (nope * scale).astype(o_ref.dtype)
        o_ref[:, base + NOPE_DIM:base + HEAD_PAD] = rr.astype(o_ref.dtype)


def _q_path(h, w_qin, q_norm_g, wq_ext, cc, ss, geom, heads):
    T, d = h.shape
    ql = w_qin.shape[1]
    tm = min(512, geom.unit)
    scale = float((NOPE_DIM + ROPE_DIM) ** -0.5 * np.log2(np.e))
    pos_spec = pl.BlockSpec((tm, V7X_LANES), lambda i: (geom.pos0_of_tile(i, tm) // tm, 0))
    return pl.pallas_call(
        functools.partial(_q_kernel, heads=heads, scale=scale),
        out_shape=jax.ShapeDtypeStruct((T, heads * HEAD_PAD), BF16),
        grid=(T // tm,),
        in_specs=[pl.BlockSpec((tm, d), lambda i: (i, 0)),
                  pl.BlockSpec((d, ql), lambda i: (0, 0)),
                  pl.BlockSpec((1, ql), lambda i: (0, 0)),
                  pl.BlockSpec((ql, heads * HEAD_PAD), lambda i: (0, 0)),
                  pos_spec, pos_spec],
        out_specs=pl.BlockSpec((tm, heads * HEAD_PAD), lambda i: (i, 0)),
        compiler_params=_params(("arbitrary",), 48 * 2**20),
        name="q_path",
    )(h, w_qin, q_norm_g.reshape(1, ql), wq_ext, cc, ss)


def _kv_kernel(h_ref, win_ref, g_ref, wk_ref, wv_ref, cc_ref, ss_ref, k_ref, v_ref, *, heads, kvl):
    y = _dot(h_ref[...], win_ref[...])
    c = y[:, :kvl]
    r = y[:, kvl:kvl + V7X_LANES]
    cn = _rms(c, g_ref[...]).astype(BF16)
    kn = _dot(cn, wk_ref[...])
    v_ref[...] = _dot(cn, wv_ref[...]).astype(v_ref.dtype)
    rr = (r * cc_ref[...] + pltpu.roll(r, 64, 1) * ss_ref[...]).astype(k_ref.dtype)
    for h in range(heads):
        base = h * HEAD_PAD
        k_ref[:, base:base + NOPE_DIM] = kn[:, h * NOPE_DIM:(h + 1) * NOPE_DIM].astype(k_ref.dtype)
        k_ref[:, base + NOPE_DIM:base + HEAD_PAD] = rr


def _kv_path(h, w_kvin, kv_norm_g, wk, wv, cc, ss, geom, heads):
    T, d = h.shape
    kvl = wk.shape[0]
    nin = w_kvin.shape[1]
    tm = min(512, geom.unit)
    pos_spec = pl.BlockSpec((tm, V7X_LANES), lambda i: (geom.pos0_of_tile(i, tm) // tm, 0))
    return pl.pallas_call(
        functools.partial(_kv_kernel, heads=heads, kvl=kvl),
        out_shape=(jax.ShapeDtypeStruct((T, heads * HEAD_PAD), BF16),
                   jax.ShapeDtypeStruct((T, heads * V_DIM), BF16)),
        grid=(T // tm,),
        in_specs=[pl.BlockSpec((tm, d), lambda i: (i, 0)),
                  pl.BlockSpec((d, nin), lambda i: (0, 0)),
                  pl.BlockSpec((1, kvl), lambda i: (0, 0)),
                  pl.BlockSpec((kvl, heads * NOPE_DIM), lambda i: (0, 0)),
                  pl.BlockSpec((kvl, heads * V_DIM), lambda i: (0, 0)),
                  pos_spec, pos_spec],
        out_specs=(pl.BlockSpec((tm, heads * HEAD_PAD), lambda i: (i, 0)),
                   pl.BlockSpec((tm, heads * V_DIM), lambda i: (i, 0))),
        compiler_params=_params(("arbitrary",), 48 * 2**20),
        name="kv_path",
    )(h, w_kvin, kv_norm_g.reshape(1, kvl), wk, wv, cc, ss)


def _attn_kernel(qblk_ref, head_ref, kunit_ref, nparts_ref, q_ref, *refs, parts, tkc, nsub):
    k_refs = refs[:parts]
    v_refs = refs[parts:2 * parts]
    o_ref = refs[2 * parts]
    w = pl.program_id(0)
    nparts = nparts_ref[w]
    tq = q_ref.shape[0]
    unit = k_refs[0].shape[0]
    nkc = unit // tkc
    ts = tq // nsub

    def make_body(k_ref, v_ref):
        def body(j, carry):
            off = pl.multiple_of(j * tkc, tkc)
            kj = k_ref[pl.ds(off, tkc), :]
            vj = v_ref[pl.ds(off, tkc), :]
            out = []
            for t in range(nsub):
                m, l, acc = carry[t]
                s = _dot_nt(q_ref[t * ts:(t + 1) * ts, :], kj)
                m_new = jnp.maximum(m, jnp.max(s, axis=-1, keepdims=True))
                alpha = jnp.exp2(m - m_new)
                p = jnp.exp2(s - m_new)
                l = alpha * l + jnp.sum(p, axis=-1, keepdims=True)
                acc = alpha * acc + _dot(p.astype(BF16), vj)
                out.append((m_new, l, acc))
            return tuple(out)
        return body

    carry = tuple((jnp.full((ts, 1), -jnp.inf, F32), jnp.zeros((ts, 1), F32), jnp.zeros((ts, V_DIM), F32))
                  for _ in range(nsub))
    for p in range(parts):
        n = jnp.where(p < nparts, nkc, 0)
        carry = lax.fori_loop(0, n, make_body(k_refs[p], v_refs[p]), carry)
    for t in range(nsub):
        _, l, acc = carry[t]
        o_ref[t * ts:(t + 1) * ts, :] = (acc / l).astype(o_ref.dtype)


def _attention(q, k, v, geom, heads):
    T = q.shape[0]
    unit = geom.unit
    parts = geom.r
    tq = min(2048, unit)
    tkc = min(1024, unit)
    nsub = 1
    qblk, head, kunit, nparts = [], [], [], []
    for b in range(geom.b1 + geom.b2):
        if b < geom.b1:
            base, slen = b * geom.s1, geom.s1
        else:
            base, slen = geom.b1 * geom.s1 + (b - geom.b1) * geom.s2, geom.s2
        for h in range(heads):
            for i in range(slen // tq):
                qblk.append(base // tq + i)
                head.append(h)
                kunit.append(base // unit)
                nparts.append(slen // unit)
    tabs = [jnp.asarray(np.asarray(t, np.int32)) for t in (qblk, head, kunit, nparts)]
    nwork = len(qblk)

    def q_map(w, qb, hd, ku, npt):
        return (qb[w], hd[w])

    def kv_map(p):
        def imap(w, qb, hd, ku, npt):
            return (ku[w] + jnp.minimum(p, npt[w] - 1), hd[w])
        return imap

    grid_spec = pltpu.PrefetchScalarGridSpec(
        num_scalar_prefetch=4,
        grid=(nwork,),
        in_specs=([pl.BlockSpec((tq, HEAD_PAD), q_map)]
                  + [pl.BlockSpec((unit, HEAD_PAD), kv_map(p)) for p in range(parts)]
                  + [pl.BlockSpec((unit, V_DIM), kv_map(p)) for p in range(parts)]),
        out_specs=pl.BlockSpec((tq, V_DIM), q_map),
    )
    return pl.pallas_call(
        functools.partial(_attn_kernel, parts=parts, tkc=tkc, nsub=nsub),
        out_shape=jax.ShapeDtypeStruct((T, heads * V_DIM), BF16),
        grid_spec=grid_spec,
        compiler_params=_params(("arbitrary",), 48 * 2**20),
        name="attention",
    )(*tabs, q, *([k] * parts), *([v] * parts))


def _sgu_kernel(u_ref, v_ref, g_ref, b_ref, ws_ref, bs_ref, o_ref, *, groups):
    tm = u_ref.shape[0]
    v = _gelu(v_ref[...].astype(F32))
    v = (_ln(v) * g_ref[...] + b_ref[...]).astype(BF16)
    for c in range(tm // CHUNK):
        rows = slice(c * CHUNK, (c + 1) * CHUNK)
        for g in range(groups):
            cols = slice(g * V7X_LANES, (g + 1) * V7X_LANES)
            mixed = _dot(ws_ref[g], v[rows, cols]) + bs_ref[:, cols]
            u = _gelu(u_ref[rows, cols].astype(F32))
            o_ref[rows, cols] = (u * mixed).astype(o_ref.dtype)


def _sgu(proj, sgu_norm_g, sgu_norm_b, ws, bs_full, geom, width):
    T = proj.shape[0]
    groups = width // V7X_LANES
    tm = min(256, geom.unit)
    return pl.pallas_call(
        functools.partial(_sgu_kernel, groups=groups),
        out_shape=jax.ShapeDtypeStruct((T, width), BF16),
        grid=(T // tm,),
        in_specs=[pl.BlockSpec((tm, width), lambda i: (i, 0)),
                  pl.BlockSpec((tm, width), lambda i: (i, 1)),
                  pl.BlockSpec((1, width), lambda i: (0, 0)),
                  pl.BlockSpec((1, width), lambda i: (0, 0)),
                  pl.BlockSpec((groups, CHUNK, CHUNK), lambda i: (0, 0, 0)),
                  pl.BlockSpec((CHUNK, width), lambda i: (0, 0))],
        out_specs=pl.BlockSpec((tm, width), lambda i: (i, 0)),
        compiler_params=_params(("arbitrary",), 32 * 2**20),
        name="sgu",
    )(proj, proj, sgu_norm_g.reshape(1, width), sgu_norm_b.reshape(1, width), ws, bs_full)


def _scan_tile(a, u, reverse):
    tm = a.shape[0]
    row = lax.broadcasted_iota(I32, (tm, 1), 0)
    s = 1
    while s < tm:
        shift = (tm - s) if reverse else s
        valid = (row < tm - s) if reverse else (row >= s)
        a_sh = jnp.where(valid, pltpu.roll(a, shift, 0), 1.0)
        u_sh = jnp.where(valid, pltpu.roll(u, shift, 0), 0.0)
        u = a * u_sh + u
        a = a * a_sh
        s *= 2
    return a, u


def _lru_kernel(*refs, reverse, blocks, geom, tm, ntiles):
    if reverse:
        (x_ref, prev_ref, next_ref, cw_ref, cb_ref, wg_ref, br_ref, bi_ref, lg_ref,
         hf_ref, rg_ref, o_ref, carry_ref) = refs
    else:
        (x_ref, prev_ref, next_ref, cw_ref, cb_ref, wg_ref, br_ref, bi_ref, lg_ref,
         o_ref, carry_ref) = refs
    step = pl.program_id(0)
    i = (ntiles - 1 - step) if reverse else step
    pos0 = geom.pos0_of_tile(i, tm)
    at_start = pos0 == 0
    at_end = pos0 + tm == geom.seqlen_of_tile(i, tm)
    x = x_ref[...].astype(F32)
    row = lax.broadcasted_iota(I32, (tm, 1), 0)
    keep_prev = jnp.where(at_start, 0.0, 1.0)
    keep_next = jnp.where(at_end, 0.0, 1.0)
    prev = prev_ref[...].astype(F32)
    p6 = prev[HALO_ROWS - 2:HALO_ROWS - 1, :] * keep_prev
    p7 = prev[HALO_ROWS - 1:HALO_ROWS, :] * keep_prev
    n0 = next_ref[...].astype(F32)[0:1, :] * keep_next
    xm1 = jnp.where(row == 0, p7, pltpu.roll(x, 1, 0))
    xm2 = jnp.where(row == 0, p6, jnp.where(row == 1, p7, pltpu.roll(x, 2, 0)))
    xp1 = jnp.where(row == tm - 1, n0, pltpu.roll(x, tm - 1, 0))
    xc = cb_ref[...] + xm2 * cw_ref[0:1, :] + xm1 * cw_ref[1:2, :] + x * cw_ref[2:3, :] + xp1 * cw_ref[3:4, :]
    xcb = xc.astype(BF16)
    rs, is_ = [], []
    for n in range(blocks):
        cols = slice(n * V7X_LANES, (n + 1) * V7X_LANES)
        g = _dot(xcb[:, cols], wg_ref[n])
        rs.append(g[:, :V7X_LANES])
        is_.append(g[:, V7X_LANES:])
    r = _sigmoid(jnp.concatenate(rs, axis=1) + br_ref[...])
    ig = _sigmoid(jnp.concatenate(is_, axis=1) + bi_ref[...])
    lg = lg_ref[...]
    softplus = jnp.maximum(-lg, 0.0) + jnp.log(1.0 + jnp.exp(-jnp.abs(lg)))
    log_a = -LRU_C * r * softplus
    a = jnp.exp(log_a)
    mult = jnp.sqrt(1.0 - a * a)
    if reverse:
        mult = jnp.where(jnp.logical_and(at_end, row == tm - 1), 1.0, mult)
    else:
        mult = jnp.where(jnp.logical_and(at_start, row == 0), 1.0, mult)
    u = mult * (ig * xc)
    acum, hloc = _scan_tile(a, u, reverse)
    fresh = at_end if reverse else at_start
    carry = jnp.where(fresh, 0.0, carry_ref[0:1, :])
    h = hloc + acum * carry
    if reverse:
        carry_ref[0:1, :] = h[0:1, :]
        o_ref[...] = ((hf_ref[...].astype(F32) + h) * _gelu(rg_ref[...].astype(F32))).astype(o_ref.dtype)
    else:
        carry_ref[0:1, :] = h[tm - 1:tm, :]
        o_ref[...] = h.astype(o_ref.dtype)


def _lru_pass(proj, xcol, gcol, conv_w, conv_b, wg, br, bi, lg, geom, width, reverse, h_fwd=None):
    T = proj.shape[0]
    tm = min(256, geom.unit)
    ntiles = T // tm
    blocks = width // V7X_LANES
    nb8 = T // HALO_ROWS
    t8 = tm // HALO_ROWS

    def tile(step):
        return (ntiles - 1 - step) if reverse else step

    in_specs = [pl.BlockSpec((tm, width), lambda s: (tile(s), xcol)),
                pl.BlockSpec((HALO_ROWS, width), lambda s: (jnp.maximum(tile(s) * t8 - 1, 0), xcol)),
                pl.BlockSpec((HALO_ROWS, width), lambda s: (jnp.minimum((tile(s) + 1) * t8, nb8 - 1), xcol)),
                pl.BlockSpec((4, width), lambda s: (0, 0)),
                pl.BlockSpec((1, width), lambda s: (0, 0)),
                pl.BlockSpec((blocks, V7X_LANES, 2 * V7X_LANES), lambda s: (0, 0, 0)),
                pl.BlockSpec((1, width), lambda s: (0, 0)),
                pl.BlockSpec((1, width), lambda s: (0, 0)),
                pl.BlockSpec((1, width), lambda s: (0, 0))]
    args = [proj, proj, proj, conv_w, conv_b.reshape(1, width), wg, br.reshape(1, width),
            bi.reshape(1, width), lg.reshape(1, width)]
    if reverse:
        in_specs += [pl.BlockSpec((tm, width), lambda s: (tile(s), 0)),
                     pl.BlockSpec((tm, width), lambda s: (tile(s), gcol))]
        args += [h_fwd, proj]
    return pl.pallas_call(
        functools.partial(_lru_kernel, reverse=reverse, blocks=blocks, geom=geom, tm=tm, ntiles=ntiles),
        out_shape=jax.ShapeDtypeStruct((T, width), BF16),
        grid=(ntiles,),
        in_specs=in_specs,
        out_specs=pl.BlockSpec((tm, width), lambda s: (tile(s), 0)),
        scratch_shapes=[pltpu.VMEM((V7X_SUBLANES, width), F32)],
        compiler_params=_params(("arbitrary",), 40 * 2**20),
        name="lru_bwd" if reverse else "lru_fwd",
    )(*args)


def _merge_kernel(oa_ref, ob_ref, oc_ref, gate_ref, w_ref, y_ref, acc_ref):
    n = pl.program_id(1)

    def branch(o_ref, first):
        p = _dot(o_ref[...], w_ref[...]) * _sigmoid(gate_ref[...].astype(F32))
        if first:
            acc_ref[...] = p
        else:
            acc_ref[...] += p

    @pl.when(n == 0)
    def _():
        branch(oa_ref, True)

    @pl.when(n == 1)
    def _():
        branch(ob_ref, False)

    @pl.when(n == 2)
    def _():
        branch(oc_ref, False)

    @pl.when(n == 3)
    def _():
        y_ref[...] = _dot(acc_ref[...].astype(BF16), w_ref[...]).astype(y_ref.dtype)


def _merge(oa, ob, oc, proj, gate_col0, w4, geom):
    T, d = oa.shape
    tm = min(512, geom.unit)
    return pl.pallas_call(
        _merge_kernel,
        out_shape=jax.ShapeDtypeStruct((T, d), BF16),
        grid=(T // tm, 4),
        in_specs=[pl.BlockSpec((tm, d), lambda i, n: (i, 0)),
                  pl.BlockSpec((tm, d), lambda i, n: (i, 0)),
                  pl.BlockSpec((tm, d), lambda i, n: (i, 0)),
                  pl.BlockSpec((tm, d), lambda i, n: (i, gate_col0 + jnp.minimum(n, 2))),
                  pl.BlockSpec((None, d, d), lambda i, n: (n, 0, 0))],
        out_specs=pl.BlockSpec((tm, d), lambda i, n: (i, 0)),
        scratch_shapes=[pltpu.VMEM((tm, d), F32)],
        compiler_params=_params(("arbitrary", "arbitrary"), 52 * 2**20),
        name="merge",
    )(oa, ob, oc, proj, w4)


def _pack_pairs(a):
    w = a.shape[1] // 2
    lo = lax.bitcast_convert_type(a[:, :w].astype(BF16).astype(F32), U32) >> 16
    hi = lax.bitcast_convert_type(a[:, w:].astype(BF16).astype(F32), U32) & np.uint32(0xFFFF0000)
    return lo | hi


def _unpack_pairs(p):
    lo = lax.bitcast_convert_type(p << 16, F32)
    hi = lax.bitcast_convert_type(p & np.uint32(0xFFFF0000), F32)
    return lo, hi


def _store_token_tiles(ref, base, packed):
    n, w = packed.shape
    nch = w // V7X_LANES
    for c in range(nch):
        ref[pl.ds(base + c, n, stride=nch), :] = packed[:, c * V7X_LANES:(c + 1) * V7X_LANES]


def _load_token_tiles(ref, base, n, nch):
    return [ref[pl.ds(base + c, n, stride=nch), :] for c in range(nch)]


def _res_mixer_kernel(x_ref, y_ref, gate_ref, g_ref, b_ref, sc_ref, sh_ref, wr_ref,
                      x1_ref, h_ref, hp_ref, lg_ref, *, alpha):
    z = alpha * x_ref[...] + gate_ref[...] * y_ref[...].astype(F32)
    x1 = _ln(z) * g_ref[...] + b_ref[...]
    x1_ref[...] = x1
    h = _ln(x1) * (1.0 + sc_ref[...]) + sh_ref[...]
    h_ref[...] = h.astype(h_ref.dtype)
    _store_token_tiles(hp_ref, 0, _pack_pairs(h))
    lg_ref[...] = _dot3_nt(wr_ref[...], h)


def _res_mixer(x, y, modr, ln_g, ln_b, w_router_t, geom, layer, alpha):
    T, d = x.shape
    ne = w_router_t.shape[0]
    tm = min(256, geom.unit)
    nch = d // 2 // V7X_LANES
    return pl.pallas_call(
        functools.partial(_res_mixer_kernel, alpha=alpha),
        out_shape=(jax.ShapeDtypeStruct((T, d), F32),
                   jax.ShapeDtypeStruct((T, d), BF16),
                   jax.ShapeDtypeStruct((T * nch, V7X_LANES), U32),
                   jax.ShapeDtypeStruct((ne, T), F32)),
        grid=(T // tm,),
        in_specs=[pl.BlockSpec((tm, d), lambda i: (i, 0)),
                  pl.BlockSpec((tm, d), lambda i: (i, 0)),
                  _mod_spec(geom, tm, layer, 2, d),
                  pl.BlockSpec((1, d), lambda i: (0, 0)),
                  pl.BlockSpec((1, d), lambda i: (0, 0)),
                  _mod_spec(geom, tm, layer, 4, d),
                  _mod_spec(geom, tm, layer, 3, d),
                  pl.BlockSpec((ne, d), lambda i: (0, 0))],
        out_specs=(pl.BlockSpec((tm, d), lambda i: (i, 0)),
                   pl.BlockSpec((tm, d), lambda i: (i, 0)),
                   pl.BlockSpec((tm * nch, V7X_LANES), lambda i: (i, 0)),
                   pl.BlockSpec((ne, tm), lambda i: (0, i))),
        compiler_params=_params(("arbitrary",), 40 * 2**20),
        name="res_mixer",
    )(x, y, modr, ln_g.reshape(1, d), ln_b.reshape(1, d), modr, modr, w_router_t)


def _res_moe_kernel(*refs, alpha, has_next):
    pair_refs, refs = refs[:TOP_K], refs[TOP_K:]
    if has_next:
        x_ref, sh_ref, gate_ref, g_ref, b_ref, sc_ref, shf_ref, x2_ref, h_ref = refs
    else:
        x_ref, sh_ref, gate_ref, g_ref, b_ref, x2_ref = refs
    tm, d = x_ref.shape
    nch = d // 2 // V7X_LANES
    los, his = [], []
    for c in range(nch):
        lo = hi = None
        for k in range(TOP_K):
            l, h = _unpack_pairs(pair_refs[k][pl.ds(c, tm, stride=nch), :])
            lo = l if lo is None else lo + l
            hi = h if hi is None else hi + h
        los.append(lo)
        his.append(hi)
    y = jnp.concatenate(los + his, axis=1) + sh_ref[...].astype(F32)
    z = alpha * x_ref[...] + gate_ref[...] * y
    x2 = _ln(z) * g_ref[...] + b_ref[...]
    x2_ref[...] = x2
    if has_next:
        h_ref[...] = (_ln(x2) * (1.0 + sc_ref[...]) + shf_ref[...]).astype(h_ref.dtype)


def _res_moe(x, pairbuf, shared, modr, ln_g, ln_b, geom, layer, alpha, has_next):
    T, d = x.shape
    tm = min(256, geom.unit)
    nch = d // 2 // V7X_LANES
    in_specs = [pl.BlockSpec((tm * nch, V7X_LANES), functools.partial(lambda k, i: (k * (T // tm) + i, 0), k))
                for k in range(TOP_K)]
    in_specs += [pl.BlockSpec((tm, d), lambda i: (i, 0)),
                 pl.BlockSpec((tm, d), lambda i: (i, 0)),
                 _mod_spec(geom, tm, layer, 5, d),
                 pl.BlockSpec((1, d), lambda i: (0, 0)),
                 pl.BlockSpec((1, d), lambda i: (0, 0))]
    args = [pairbuf] * TOP_K + [x, shared, modr, ln_g.reshape(1, d), ln_b.reshape(1, d)]
    out_shape = [jax.ShapeDtypeStruct((T, d), F32)]
    out_specs = [pl.BlockSpec((tm, d), lambda i: (i, 0))]
    if has_next:
        in_specs += [_mod_spec(geom, tm, layer + 1, 1, d), _mod_spec(geom, tm, layer + 1, 0, d)]
        args += [modr, modr]
        out_shape.append(jax.ShapeDtypeStruct((T, d), BF16))
        out_specs.append(pl.BlockSpec((tm, d), lambda i: (i, 0)))
    return pl.pallas_call(
        functools.partial(_res_moe_kernel, alpha=alpha, has_next=has_next),
        out_shape=tuple(out_shape),
        grid=(T // tm,),
        in_specs=in_specs,
        out_specs=tuple(out_specs),
        compiler_params=_params(("arbitrary",), 40 * 2**20),
        name="res_moe",
    )(*args)


def _argmax_rows(vals, row, nrow):
    m = jnp.max(vals, axis=0, keepdims=True)
    idx = jnp.min(jnp.where(vals == m, row, nrow), axis=0, keepdims=True)
    return m, idx


def _route_kernel(lg_ref, bias_ref, idx_ref, w_ref):
    ne, tr = lg_ref.shape
    gsz = ne // N_GROUPS
    scores = _sigmoid(lg_ref[...])
    biased = scores + bias_ref[...]
    neg = -jnp.inf
    grow = lax.broadcasted_iota(I32, (gsz, tr), 0)
    gscore = []
    for g in range(N_GROUPS):
        blk = biased[g * gsz:(g + 1) * gsz, :]
        m1, i1 = _argmax_rows(blk, grow, gsz)
        m2 = jnp.max(jnp.where(grow == i1, neg, blk), axis=0, keepdims=True)
        gscore.append(m1 + m2)
    gs = jnp.concatenate(gscore, axis=0)
    gidx = lax.broadcasted_iota(I32, (N_GROUPS, tr), 0)
    keep = jnp.zeros((N_GROUPS, tr), F32)
    for _ in range(TOPK_GROUPS):
        _, gi = _argmax_rows(gs, gidx, N_GROUPS)
        sel = gidx == gi
        keep = jnp.where(sel, 1.0, keep)
        gs = jnp.where(sel, neg, gs)
    keep_full = jnp.concatenate(
        [jnp.broadcast_to(keep[g:g + 1, :], (gsz, tr)) for g in range(N_GROUPS)], axis=0)
    masked = jnp.where(keep_full > 0.5, biased, neg)
    row = lax.broadcasted_iota(I32, (ne, tr), 0)
    idxs, ws = [], []
    for _ in range(TOP_K):
        _, ei = _argmax_rows(masked, row, ne)
        sel = row == ei
        ws.append(jnp.sum(jnp.where(sel, scores, 0.0), axis=0, keepdims=True))
        idxs.append(ei)
        masked = jnp.where(sel, neg, masked)
    wk = jnp.concatenate(ws, axis=0)
    idx_ref[...] = jnp.concatenate(idxs, axis=0)
    w_ref[...] = wk / jnp.sum(wk, axis=0, keepdims=True) * ROUTED_SCALE


def _route(logits_t, router_bias):
    ne, T = logits_t.shape
    tr = min(512, T)
    return pl.pallas_call(
        _route_kernel,
        out_shape=(jax.ShapeDtypeStruct((TOP_K, T), I32), jax.ShapeDtypeStruct((TOP_K, T), F32)),
        grid=(T // tr,),
        in_specs=[pl.BlockSpec((ne, tr), lambda i: (0, i)),
                  pl.BlockSpec((ne, 1), lambda i: (0, 0))],
        out_specs=(pl.BlockSpec((TOP_K, tr), lambda i: (0, i)),
                   pl.BlockSpec((TOP_K, tr), lambda i: (0, i))),
        compiler_params=_params(("arbitrary",), 32 * 2**20),
        name="route",
    )(logits_t, router_bias.reshape(ne, 1))


def _expert_kernel(be_ref, nact_ref,
                   tok0_ref, tok1_ref, tok2_ref, dst_ref, w_ref, x_hbm, wg_ref, wu_ref, wd_ref,
                   out_hbm,
                   xbuf, ybuf, xmat, wgb, wub, wdb, gsem, ssem, *, rows, nch):
    b = pl.program_id(0)
    nact = nact_ref[0]
    slot = b % 2
    xslot = b % 3
    span = rows * nch

    def gather_start(idx_ref, sl):
        base = sl * span

        def body(t, c):
            for k in range(DMA_UNROLL):
                r = t * DMA_UNROLL + k
                src = x_hbm.at[pl.ds(pl.multiple_of(idx_ref[0, r], nch), nch)]
                dst = xbuf.at[pl.ds(pl.multiple_of(base + r * nch, nch), nch)]
                pltpu.make_async_copy(src, dst, gsem.at[sl]).start()
            return c
        lax.fori_loop(0, rows // DMA_UNROLL, body, 0)

    def gather_wait(sl):
        base = pl.multiple_of(sl * span, span)
        pltpu.make_async_copy(x_hbm.at[pl.ds(0, span)], xbuf.at[pl.ds(base, span)], gsem.at[sl]).wait()

    def scatter_start(sl):
        base = sl * span

        def body(t, c):
            for k in range(DMA_UNROLL):
                r = t * DMA_UNROLL + k
                src = ybuf.at[pl.ds(pl.multiple_of(base + r * nch, nch), nch)]
                dst = out_hbm.at[pl.ds(pl.multiple_of(dst_ref[0, r], nch), nch)]
                pltpu.make_async_copy(src, dst, ssem.at[sl]).start()
            return c
        lax.fori_loop(0, rows // DMA_UNROLL, body, 0)

    def scatter_wait(sl):
        base = pl.multiple_of(sl * span, span)
        pltpu.make_async_copy(ybuf.at[pl.ds(base, span)], out_hbm.at[pl.ds(0, span)], ssem.at[sl]).wait()

    @pl.when(jnp.logical_and(b == 0, nact > 0))
    def _():
        gather_start(tok0_ref, 0)

    @pl.when(jnp.logical_and(b == 0, nact > 1))
    def _():
        gather_start(tok1_ref, 1)

    @pl.when(b < nact)
    def _():
        prev_e = be_ref[jnp.maximum(b - 1, 0)]

        @pl.when(jnp.logical_or(b == 0, be_ref[b] != prev_e))
        def _():
            wgb[...] = wg_ref[...].astype(BF16)
            wub[...] = wu_ref[...].astype(BF16)
            wdb[...] = wd_ref[...].astype(BF16)

        gather_wait(xslot)
        for c, p in enumerate(_load_token_tiles(xbuf, xslot * span, rows, nch)):
            lo, hi = _unpack_pairs(p)
            xmat[:, c * V7X_LANES:(c + 1) * V7X_LANES] = lo.astype(BF16)
            xmat[:, (nch + c) * V7X_LANES:(nch + c + 1) * V7X_LANES] = hi.astype(BF16)
        x = xmat[...]
        g = _dot(x, wgb[...])
        u = _dot(x, wub[...])
        wrow = w_ref[...]
        eye = lax.broadcasted_iota(I32, (rows, rows), 0) == lax.broadcasted_iota(I32, (rows, rows), 1)
        wcol = jnp.sum(jnp.where(eye, wrow, 0.0), axis=1, keepdims=True)
        hmid = (g * _sigmoid(g)) * (u * wcol)
        y = _dot(hmid.astype(BF16), wdb[...])

        @pl.when(b >= 2)
        def _():
            scatter_wait(slot)

        _store_token_tiles(ybuf, slot * span, _pack_pairs(y))
        scatter_start(slot)

        @pl.when(b + 2 < nact)
        def _():
            gather_start(tok2_ref, (b + 2) % 3)

        @pl.when(b == nact - 1)
        def _():
            scatter_wait(slot)

            @pl.when(b >= 1)
            def _():
                scatter_wait(1 - slot)


def _experts(hp, block_expert, nact, slot_tok, slot_dst, slot_w, w_e_gate, w_e_up, w_e_down, layer):
    d, de = w_e_gate.shape[-2:]
    nch = d // 2 // V7X_LANES
    T = hp.shape[0] // nch
    nblk = block_expert.shape[0]
    rows = EXPERT_ROWS
    tok3 = (slot_tok * nch).reshape(nblk, 1, rows)
    dst3 = (slot_dst * nch).reshape(nblk, 1, rows)
    w3 = slot_w.reshape(nblk, 1, rows)

    def emap(b, be, na):
        return (layer, be[jnp.minimum(b, jnp.maximum(na[0] - 1, 0))], 0, 0)

    def tok_spec(ahead):
        return pl.BlockSpec((None, 1, rows), lambda b, be, na: (jnp.minimum(b + ahead, nblk - 1), 0, 0),
                            memory_space=pltpu.SMEM)

    grid_spec = pltpu.PrefetchScalarGridSpec(
        num_scalar_prefetch=2,
        grid=(nblk,),
        in_specs=[tok_spec(0), tok_spec(1), tok_spec(2),
                  pl.BlockSpec((None, 1, rows), lambda b, be, na: (b, 0, 0), memory_space=pltpu.SMEM),
                  pl.BlockSpec((None, 1, rows), lambda b, be, na: (b, 0, 0)),
                  pl.BlockSpec(memory_space=pl.ANY),
                  pl.BlockSpec((None, None, d, de), emap),
                  pl.BlockSpec((None, None, d, de), emap),
                  pl.BlockSpec((None, None, de, d), emap)],
        out_specs=pl.BlockSpec(memory_space=pl.ANY),
        scratch_shapes=[pltpu.VMEM((3 * rows * nch, V7X_LANES), U32),
                        pltpu.VMEM((2 * rows * nch, V7X_LANES), U32),
                        pltpu.VMEM((rows, d), BF16),
                        pltpu.VMEM((d, de), BF16),
                        pltpu.VMEM((d, de), BF16),
                        pltpu.VMEM((de, d), BF16),
                        pltpu.SemaphoreType.DMA((3,)),
                        pltpu.SemaphoreType.DMA((2,))],
    )
    return pl.pallas_call(
        functools.partial(_expert_kernel, rows=rows, nch=nch),
        out_shape=jax.ShapeDtypeStruct(((TOP_K * T + 2 * rows) * nch, V7X_LANES), U32),
        grid_spec=grid_spec,
        compiler_params=_params(("arbitrary",), 52 * 2**20),
        name="experts",
    )(block_expert, nact, tok3, tok3, tok3, dst3, w3, hp, w_e_gate, w_e_up, w_e_down)


def _dispatch_plan(top_idx_t, top_w_t, n_experts):
    k, T = top_idx_t.shape
    rows = EXPERT_ROWS
    n_pairs = k * T
    pair_expert = top_idx_t.reshape(n_pairs)
    pair_w = top_w_t.reshape(n_pairs)
    order = jnp.argsort(pair_expert).astype(I32)
    bounds = jnp.searchsorted(pair_expert[order], jnp.arange(n_experts + 1, dtype=I32), side='left').astype(I32)
    start = bounds[:-1]
    counts = bounds[1:] - start
    padded = (counts + rows - 1) // rows * rows
    padded_end = jnp.cumsum(padded)
    padded_start = padded_end - padded
    nblk = n_pairs // rows + n_experts
    block_start = jnp.arange(nblk, dtype=I32) * rows
    block_expert = jnp.minimum(jnp.searchsorted(padded_end, block_start, side='right'), n_experts - 1).astype(I32)
    nact = (padded_end[-1] // rows).astype(I32).reshape(1)
    off = block_start - padded_start[block_expert]
    r = jnp.arange(rows, dtype=I32)[None, :]
    j = off[:, None] + r
    valid = j < counts[block_expert][:, None]
    src = jnp.clip(start[block_expert][:, None] + j, 0, n_pairs - 1)
    slot_pair = jnp.where(valid, order[src], 0)
    slot_tok = slot_pair % T
    slot_w = jnp.where(valid, pair_w[slot_pair], 0.0)
    spare = n_pairs + (jnp.arange(nblk, dtype=I32)[:, None] % 2) * rows + r
    slot_dst = jnp.where(valid, slot_pair, spare)
    return block_expert, nact, slot_tok.astype(I32), slot_dst.astype(I32), slot_w.astype(F32)


def _shared_kernel(h_ref, wgu_ref, wd_ref, o_ref):
    gu = _dot(h_ref[...], wgu_ref[...])
    ds = gu.shape[1] // 2
    g, u = gu[:, :ds], gu[:, ds:]
    o_ref[...] = _dot(((g * _sigmoid(g)) * u).astype(BF16), wd_ref[...]).astype(o_ref.dtype)


def _shared(h, wgu, wd, geom):
    T, d = h.shape
    tm = min(1024, geom.unit)
    return pl.pallas_call(
        _shared_kernel,
        out_shape=jax.ShapeDtypeStruct((T, d), BF16),
        grid=(T // tm,),
        in_specs=[pl.BlockSpec((tm, d), lambda i: (i, 0)),
                  pl.BlockSpec(wgu.shape, lambda i: (0, 0)),
                  pl.BlockSpec(wd.shape, lambda i: (0, 0))],
        out_specs=pl.BlockSpec((tm, d), lambda i: (i, 0)),
        compiler_params=_params(("arbitrary",), 40 * 2**20),
        name="shared_expert",
    )(h, wgu, wd)


def _rope_tables(s):
    half = ROPE_DIM // 2
    inv_freq = ROPE_THETA ** (-jnp.arange(half, dtype=F32) / half)
    ang = jnp.arange(s, dtype=F32)[:, None] * inv_freq[None, :]
    cos, sin = jnp.cos(ang), jnp.sin(ang)
    z = jnp.zeros((s, V7X_LANES - ROPE_DIM), F32)
    cc = jnp.concatenate([cos, cos, z], axis=1)
    ss = jnp.concatenate([-sin, sin, z], axis=1)
    return cc, ss


def _swap_halves(w):
    half = w.shape[-1] // 2
    return jnp.concatenate([w[..., half:], w[..., :half]], axis=-1)


def kernel(x_prompt, x_sample, c_prompt, c_sample, w_mod, b_mod, w_in, q_norm_g, w_uq, kv_norm_g, w_ukv, sgu_norm_g, sgu_norm_b, w_spatial, b_spatial, conv_w, conv_b, w_rgate, b_rgate, w_igate, b_igate, lru_logit, w_branch, w_out, ln1_g, ln1_b, w_router, router_bias, w_e_gate, w_e_up, w_e_down, w_sh_gate, w_sh_up, w_sh_down, ln2_g, ln2_b):
    b1, s1, d = x_prompt.shape
    b2, s2, _ = x_sample.shape
    geom = _Geom(b1, s1, b2, s2)
    T = geom.T
    L = w_mod.shape[0]
    alpha = float((2 * L) ** 0.25)
    qlr = q_norm_g.shape[1]
    kvl = kv_norm_g.shape[1]
    heads = w_uq.shape[2] // (NOPE_DIM + ROPE_DIM)
    width = sgu_norm_g.shape[1]
    n_experts = w_router.shape[2]
    assert geom.nrows <= 8 and width == d and heads * V_DIM == d

    x = jnp.concatenate([x_prompt.reshape(b1 * s1, d), x_sample.reshape(b2 * s2, d)], axis=0)
    c8 = jnp.concatenate([c_prompt, c_sample, jnp.zeros((8 - geom.nrows, d), F32)], axis=0)
    modr = _mod_all(c8, w_mod, b_mod).reshape(L * 8 * 6, 1, d)
    cc, ss = _rope_tables(s1)

    o_q, o_kv, o_kr = 0, qlr, qlr + kvl
    o_gm = o_kr + ROPE_DIM
    o_rx = o_gm + 2 * width
    o_rg = o_rx + width
    o_gl = o_rg + width

    h = _lnmod(x, modr, geom, 0)
    for l in range(L):
        wl = w_in[l]
        w_qin = wl[:, o_q:o_kv].astype(BF16)
        w_kr = wl[:, o_kr:o_gm]
        w_kvin = jnp.concatenate([wl[:, o_kv:o_kr], w_kr, _swap_halves(w_kr)], axis=1).astype(BF16)
        w_rest = wl[:, o_gm:].astype(BF16)
        uq = w_uq[l].reshape(qlr, heads, NOPE_DIM + ROPE_DIM)
        wq_ext = jnp.concatenate([uq, _swap_halves(uq[:, :, NOPE_DIM:])], axis=2).reshape(qlr, heads * HEAD_PAD).astype(BF16)
        ukv = w_ukv[l].reshape(kvl, heads, NOPE_DIM + V_DIM)
        wk = ukv[:, :, :NOPE_DIM].reshape(kvl, heads * NOPE_DIM).astype(BF16)
        wv = ukv[:, :, NOPE_DIM:].reshape(kvl, heads * V_DIM).astype(BF16)

        q = _q_path(h, w_qin, q_norm_g[l], wq_ext, cc, ss, geom, heads)
        kk, vv = _kv_path(h, w_kvin, kv_norm_g[l], wk, wv, cc, ss, geom, heads)
        o_a = _attention(q, kk, vv, geom, heads)
        proj = _matmul(h, w_rest, min(1024, geom.unit), width // 2, name="in_proj")
        ws = w_spatial[l].astype(BF16)
        bs_full = jnp.repeat(b_spatial[l].T, V7X_LANES, axis=1)
        o_b = _sgu(proj, sgu_norm_g[l], sgu_norm_b[l], ws, bs_full, geom, width)
        xcol, gcol = 2, 3
        hf = None
        for dr in range(2):
            wg = jnp.concatenate([w_rgate[l, dr], w_igate[l, dr]], axis=2).astype(BF16)
            out = _lru_pass(proj, xcol, gcol, conv_w[l], conv_b[l], wg, b_rgate[l, dr], b_igate[l, dr],
                            lru_logit[l, dr], geom, width, reverse=(dr == 1), h_fwd=hf)
            hf = out
        o_c = hf
        w4 = jnp.concatenate([w_branch[l], w_out[l][None]], axis=0).astype(BF16)
        y = _merge(o_a, o_b, o_c, proj, 4, w4, geom)
        x, h2, hp, logits_t = _res_mixer(x, y, modr, ln1_g[l], ln1_b[l], w_router[l].T, geom, l, alpha)

        top_idx_t, top_w_t = _route(logits_t, router_bias[l])
        plan = _dispatch_plan(top_idx_t, top_w_t, n_experts)
        pairbuf = _experts(hp, *plan, w_e_gate, w_e_up, w_e_down, l)
        wgu = jnp.concatenate([w_sh_gate[l], w_sh_up[l]], axis=1).astype(BF16)
        shared = _shared(h2, wgu, w_sh_down[l].astype(BF16), geom)
        outs = _res_moe(x, pairbuf, shared, modr, ln2_g[l], ln2_b[l], geom, l, alpha, has_next=(l + 1 < L))
        if l + 1 < L:
            x, h = outs
        else:
            (x,) = outs

    y_prompt = x[:b1 * s1].reshape(b1, s1, d)
    y_sample = x[b1 * s1:].reshape(b2, s2, d)
    return (y_prompt, y_sample)
```

```python
import functools

import numpy as np
import jax
import jax.numpy as jnp
from jax import lax
from jax.experimental import pallas as pl
from jax.experimental.pallas import tpu as pltpu

F32 = jnp.float32
BF16 = jnp.bfloat16
U32 = jnp.uint32
I32 = jnp.int32

V7X_LANES = 128
V7X_SUBLANES = 8
V7X_VMEM_BYTES = 64 * 1024 * 1024

LN_EPS = 1e-5
RMS_EPS = 1e-6
ROPE_THETA = 10000.0
ROPE_DIM = 64
NOPE_DIM = 128
V_DIM = 128
HEAD_PAD = 256
CHUNK = 128
LRU_C = 8.0
TOP_K = 8
N_GROUPS = 8
TOPK_GROUPS = 4
ROUTED_SCALE = 2.5
EXPERT_ROWS = 256
HALO_ROWS = 16
DMA_UNROLL = 8


def _params(sem, vmem_bytes):
    assert vmem_bytes <= V7X_VMEM_BYTES - 4 * 1024 * 1024
    return pltpu.CompilerParams(dimension_semantics=sem, vmem_limit_bytes=int(vmem_bytes))


class _Geom:
    def __init__(self, b1, s1, b2, s2):
        assert s1 % s2 == 0
        self.b1, self.s1, self.b2, self.s2 = b1, s1, b2, s2
        self.unit = s2
        self.r = s1 // s2
        self.n1 = b1 * self.r
        self.T = b1 * s1 + b2 * s2
        self.nrows = b1 + b2

    def unit_of_tile(self, i, tm):
        return i // (self.unit // tm)

    def row_of_tile(self, i, tm):
        u = self.unit_of_tile(i, tm)
        return jnp.where(u < self.n1, u // self.r, self.b1 + u - self.n1)

    def pos0_of_tile(self, i, tm):
        u = self.unit_of_tile(i, tm)
        return jnp.where(u < self.n1, (i * tm) % self.s1, (i * tm) % self.s2)

    def seqlen_of_tile(self, i, tm):
        u = self.unit_of_tile(i, tm)
        return jnp.where(u < self.n1, self.s1, self.s2)


def _mod_spec(geom, tm, layer, chunk, d):
    def imap(i, *_):
        return (layer * 8 * 6 + geom.row_of_tile(i, tm) * 6 + chunk, 0, 0)
    return pl.BlockSpec((None, 1, d), imap)


def _ln(x):
    mu = jnp.mean(x, axis=-1, keepdims=True)
    xc = x - mu
    var = jnp.mean(xc * xc, axis=-1, keepdims=True)
    return xc * lax.rsqrt(var + LN_EPS)


def _rms(x, g):
    return x * lax.rsqrt(jnp.mean(x * x, axis=-1, keepdims=True) + RMS_EPS) * g


def _gelu(x):
    return 0.5 * x * (1.0 + lax.erf(x * np.float32(1.0 / np.sqrt(2.0))))


def _sigmoid(x):
    return 1.0 / (1.0 + jnp.exp(-x))


def _dot(a, b):
    return jnp.dot(a, b, preferred_element_type=F32)


def _dot_nt(a, b):
    return lax.dot_general(a, b, (((1,), (1,)), ((), ())), preferred_element_type=F32)


def _split(a):
    hi = a.astype(BF16)
    lo = (a - hi.astype(F32)).astype(BF16)
    return hi, lo


def _dot3(a, b):
    ah, al = _split(a)
    bh, bl = _split(b)
    return _dot(ah, bh) + (_dot(ah, bl) + _dot(al, bh))


def _dot3_nt(a, b):
    ah, al = _split(a)
    bh, bl = _split(b)
    return _dot_nt(ah, bh) + (_dot_nt(ah, bl) + _dot_nt(al, bh))


def _mod_kernel(c_ref, w_ref, b_ref, o_ref):
    c = c_ref[...]
    s = c * _sigmoid(c)
    o_ref[...] = _dot3(s, w_ref[...]) + b_ref[...]


def _mod_all(c8, w_mod, b_mod):
    L, d, n = w_mod.shape
    tn = min(n, 1536)
    assert n % tn == 0
    return pl.pallas_call(
        _mod_kernel,
        out_shape=jax.ShapeDtypeStruct((L, 8, n), F32),
        grid=(L, n // tn),
        in_specs=[pl.BlockSpec((8, d), lambda l, j: (0, 0)),
                  pl.BlockSpec((None, d, tn), lambda l, j: (l, 0, j)),
                  pl.BlockSpec((None, 1, tn), lambda l, j: (l, 0, j))],
        out_specs=pl.BlockSpec((None, 8, tn), lambda l, j: (l, 0, j)),
        compiler_params=_params(("arbitrary", "arbitrary"), 48 * 2**20),
        name="mod",
    )(c8, w_mod, b_mod.reshape(L, 1, n))


def _lnmod_kernel(x_ref, sc_ref, sh_ref, o_ref):
    o_ref[...] = (_ln(x_ref[...]) * (1.0 + sc_ref[...]) + sh_ref[...]).astype(o_ref.dtype)


def _lnmod(x, modr, geom, layer):
    T, d = x.shape
    tm = min(512, geom.unit)
    return pl.pallas_call(
        _lnmod_kernel,
        out_shape=jax.ShapeDtypeStruct((T, d), BF16),
        grid=(T // tm,),
        in_specs=[pl.BlockSpec((tm, d), lambda i: (i, 0)),
                  _mod_spec(geom, tm, layer, 1, d),
                  _mod_spec(geom, tm, layer, 0, d)],
        out_specs=pl.BlockSpec((tm, d), lambda i: (i, 0)),
        compiler_params=_params(("arbitrary",), 32 * 2**20),
        name="lnmod",
    )(x, modr, modr)


def _mm_kernel(a_ref, b_ref, o_ref):
    o_ref[...] = _dot(a_ref[...], b_ref[...]).astype(o_ref.dtype)


def _matmul(a, b, tm, tn, out_dtype=BF16, name="mm"):
    M, K = a.shape
    _, N = b.shape
    tm, tn = min(tm, M), min(tn, N)
    assert M % tm == 0 and N % tn == 0
    return pl.pallas_call(
        _mm_kernel,
        out_shape=jax.ShapeDtypeStruct((M, N), out_dtype),
        grid=(M // tm, N // tn),
        in_specs=[pl.BlockSpec((tm, K), lambda i, j: (i, 0)),
                  pl.BlockSpec((K, tn), lambda i, j: (0, j))],
        out_specs=pl.BlockSpec((tm, tn), lambda i, j: (i, j)),
        compiler_params=_params(("arbitrary", "arbitrary"), 48 * 2**20),
        name=name,
    )(a, b)


def _q_kernel(h_ref, win_ref, g_ref, w_ref, cc_ref, ss_ref, o_ref, *, heads, scale):
    ql = _dot(h_ref[...], win_ref[...])
    qn = _rms(ql, g_ref[...]).astype(BF16)
    y = _dot(qn, w_ref[...])
    cc = cc_ref[...] * scale
    ss = ss_ref[...] * scale
    for h in range(heads):
        base = h * HEAD_PAD
        nope = y[:, base:base + NOPE_DIM]
        r = y[:, base + NOPE_DIM:base + HEAD_PAD]
        rr = r * cc + pltpu.roll(r, 64, 1) * ss
        o_ref[:, base:base + NOPE_DIM] = (nope * scale).astype(o_ref.dtype)
        o_ref[:, base + NOPE_DIM:base + HEAD_PAD] = rr.astype(o_ref.dtype)


def _q_path(h, w_qin, q_norm_g, wq_ext, cc, ss, geom, heads):
    T, d = h.shape
    ql = w_qin.shape[1]
    tm = min(512, geom.unit)
    scale = float((NOPE_DIM + ROPE_DIM) ** -0.5 * np.log2(np.e))
    pos_spec = pl.BlockSpec((tm, V7X_LANES), lambda i: (geom.pos0_of_tile(i, tm) // tm, 0))
    return pl.pallas_call(
        functools.partial(_q_kernel, heads=heads, scale=scale),
        out_shape=jax.ShapeDtypeStruct((T, heads * HEAD_PAD), BF16),
        grid=(T // tm,),
        in_specs=[pl.BlockSpec((tm, d), lambda i: (i, 0)),
                  pl.BlockSpec((d, ql), lambda i: (0, 0)),
                  pl.BlockSpec((1, ql), lambda i: (0, 0)),
                  pl.BlockSpec((ql, heads * HEAD_PAD), lambda i: (0, 0)),
                  pos_spec, pos_spec],
        out_specs=pl.BlockSpec((tm, heads * HEAD_PAD), lambda i: (i, 0)),
        compiler_params=_params(("arbitrary",), 48 * 2**20),
        name="q_path",
    )(h, w_qin, q_norm_g.reshape(1, ql), wq_ext, cc, ss)


def _kv_kernel(h_ref, win_ref, g_ref, wk_ref, wv_ref, cc_ref, ss_ref, k_ref, v_ref, *, heads, kvl):
    y = _dot(h_ref[...], win_ref[...])
    c = y[:, :kvl]
    r = y[:, kvl:kvl + V7X_LANES]
    cn = _rms(c, g_ref[...]).astype(BF16)
    kn = _dot(cn, wk_ref[...])
    v_ref[...] = _dot(cn, wv_ref[...]).astype(v_ref.dtype)
    rr = (r * cc_ref[...] + pltpu.roll(r, 64, 1) * ss_ref[...]).astype(k_ref.dtype)
    for h in range(heads):
        base = h * HEAD_PAD
        k_ref[:, base:base + NOPE_DIM] = kn[:, h * NOPE_DIM:(h + 1) * NOPE_DIM].astype(k_ref.dtype)
        k_ref[:, base + NOPE_DIM:base + HEAD_PAD] = rr


def _kv_path(h, w_kvin, kv_norm_g, wk, wv, cc, ss, geom, heads):
    T, d = h.shape
    kvl = wk.shape[0]
    nin = w_kvin.shape[1]
    tm = min(512, geom.unit)
    pos_spec = pl.BlockSpec((tm, V7X_LANES), lambda i: (geom.pos0_of_tile(i, tm) // tm, 0))
    return pl.pallas_call(
        functools.partial(_kv_kernel, heads=heads, kvl=kvl),
        out_shape=(jax.ShapeDtypeStruct((T, heads * HEAD_PAD), BF16),
                   jax.ShapeDtypeStruct((T, heads * V_DIM), BF16)),
        grid=(T // tm,),
        in_specs=[pl.BlockSpec((tm, d), lambda i: (i, 0)),
                  pl.BlockSpec((d, nin), lambda i: (0, 0)),
                  pl.BlockSpec((1, kvl), lambda i: (0, 0)),
                  pl.BlockSpec((kvl, heads * NOPE_DIM), lambda i: (0, 0)),
                  pl.BlockSpec((kvl, heads * V_DIM), lambda i: (0, 0)),
                  pos_spec, pos_spec],
        out_specs=(pl.BlockSpec((tm, heads * HEAD_PAD), lambda i: (i, 0)),
                   pl.BlockSpec((tm, heads * V_DIM), lambda i: (i, 0))),
        compiler_params=_params(("arbitrary",), 48 * 2**20),
        name="kv_path",
    )(h, w_kvin, kv_norm_g.reshape(1, kvl), wk, wv, cc, ss)


def _attn_kernel(qblk_ref, head_ref, kunit_ref, nparts_ref, q_ref, *refs, parts, tkc, nsub):
    k_refs = refs[:parts]
    v_refs = refs[parts:2 * parts]
    o_ref = refs[2 * parts]
    w = pl.program_id(0)
    nparts = nparts_ref[w]
    tq = q_ref.shape[0]
    unit = k_refs[0].shape[0]
    nkc = unit // tkc
    ts = tq // nsub

    def make_body(k_ref, v_ref):
        def body(j, carry):
            off = pl.multiple_of(j * tkc, tkc)
            kj = k_ref[pl.ds(off, tkc), :]
            vj = v_ref[pl.ds(off, tkc), :]
            out = []
            for t in range(nsub):
                m, l, acc = carry[t]
                s = _dot_nt(q_ref[t * ts:(t + 1) * ts, :], kj)
                m_new = jnp.maximum(m, jnp.max(s, axis=-1, keepdims=True))
                alpha = jnp.exp2(m - m_new)
                p = jnp.exp2(s - m_new)
                l = alpha * l + jnp.sum(p, axis=-1, keepdims=True)
                acc = alpha * acc + _dot(p.astype(BF16), vj)
                out.append((m_new, l, acc))
            return tuple(out)
        return body

    carry = tuple((jnp.full((ts, 1), -jnp.inf, F32), jnp.zeros((ts, 1), F32), jnp.zeros((ts, V_DIM), F32))
                  for _ in range(nsub))
    for p in range(parts):
        n = jnp.where(p < nparts, nkc, 0)
        carry = lax.fori_loop(0, n, make_body(k_refs[p], v_refs[p]), carry)
    for t in range(nsub):
        _, l, acc = carry[t]
        o_ref[t * ts:(t + 1) * ts, :] = (acc / l).astype(o_ref.dtype)


def _attention(q, k, v, geom, heads):
    T = q.shape[0]
    unit = geom.unit
    parts = geom.r
    tq = min(2048, unit)
    tkc = min(1024, unit)
    nsub = 1
    qblk, head, kunit, nparts = [], [], [], []
    for b in range(geom.b1 + geom.b2):
        if b < geom.b1:
            base, slen = b * geom.s1, geom.s1
        else:
            base, slen = geom.b1 * geom.s1 + (b - geom.b1) * geom.s2, geom.s2
        for h in range(heads):
            for i in range(slen // tq):
                qblk.append(base // tq + i)
                head.append(h)
                kunit.append(base // unit)
                nparts.append(slen // unit)
    tabs = [jnp.asarray(np.asarray(t, np.int32)) for t in (qblk, head, kunit, nparts)]
    nwork = len(qblk)

    def q_map(w, qb, hd, ku, npt):
        return (qb[w], hd[w])

    def kv_map(p):
        def imap(w, qb, hd, ku, npt):
            return (ku[w] + jnp.minimum(p, npt[w] - 1), hd[w])
        return imap

    grid_spec = pltpu.PrefetchScalarGridSpec(
        num_scalar_prefetch=4,
        grid=(nwork,),
        in_specs=([pl.BlockSpec((tq, HEAD_PAD), q_map)]
                  + [pl.BlockSpec((unit, HEAD_PAD), kv_map(p)) for p in range(parts)]
                  + [pl.BlockSpec((unit, V_DIM), kv_map(p)) for p in range(parts)]),
        out_specs=pl.BlockSpec((tq, V_DIM), q_map),
    )
    return pl.pallas_call(
        functools.partial(_attn_kernel, parts=parts, tkc=tkc, nsub=nsub),
        out_shape=jax.ShapeDtypeStruct((T, heads * V_DIM), BF16),
        grid_spec=grid_spec,
        compiler_params=_params(("arbitrary",), 48 * 2**20),
        name="attention",
    )(*tabs, q, *([k] * parts), *([v] * parts))


def _sgu_kernel(u_ref, v_ref, g_ref, b_ref, ws_ref, bs_ref, o_ref, *, groups):
    tm = u_ref.shape[0]
    v = _gelu(v_ref[...].astype(F32))
    v = (_ln(v) * g_ref[...] + b_ref[...]).astype(BF16)
    for c in range(tm // CHUNK):
        rows = slice(c * CHUNK, (c + 1) * CHUNK)
        for g in range(groups):
            cols = slice(g * V7X_LANES, (g + 1) * V7X_LANES)
            mixed = _dot(ws_ref[g], v[rows, cols]) + bs_ref[:, cols]
            u = _gelu(u_ref[rows, cols].astype(F32))
            o_ref[rows, cols] = (u * mixed).astype(o_ref.dtype)


def _sgu(proj, sgu_norm_g, sgu_norm_b, ws, bs_full, geom, width):
    T = proj.shape[0]
    groups = width // V7X_LANES
    tm = min(256, geom.unit)
    return pl.pallas_call(
        functools.partial(_sgu_kernel, groups=groups),
        out_shape=jax.ShapeDtypeStruct((T, width), BF16),
        grid=(T // tm,),
        in_specs=[pl.BlockSpec((tm, width), lambda i: (i, 0)),
                  pl.BlockSpec((tm, width), lambda i: (i, 1)),
                  pl.BlockSpec((1, width), lambda i: (0, 0)),
                  pl.BlockSpec((1, width), lambda i: (0, 0)),
                  pl.BlockSpec((groups, CHUNK, CHUNK), lambda i: (0, 0, 0)),
                  pl.BlockSpec((CHUNK, width), lambda i: (0, 0))],
        out_specs=pl.BlockSpec((tm, width), lambda i: (i, 0)),
        compiler_params=_params(("arbitrary",), 32 * 2**20),
        name="sgu",
    )(proj, proj, sgu_norm_g.reshape(1, width), sgu_norm_b.reshape(1, width), ws, bs_full)


def _scan_tile(a, u, carry, reverse):
    tm = a.shape[0]
    sub = lax.broadcasted_iota(I32, (tm, 1), 0) % V7X_SUBLANES
    s = 1
    while s < V7X_SUBLANES:
        shift = (tm - s) if reverse else s
        valid = (sub < V7X_SUBLANES - s) if reverse else (sub >= s)
        a_sh = jnp.where(valid, pltpu.roll(a, shift, 0), 1.0)
        u_sh = jnp.where(valid, pltpu.roll(u, shift, 0), 0.0)
        u = a * u_sh + u
        a = a * a_sh
        s *= 2
    ngroups = tm // V7X_SUBLANES
    hs = [None] * ngroups
    for g in (range(ngroups - 1, -1, -1) if reverse else range(ngroups)):
        rows = slice(g * V7X_SUBLANES, (g + 1) * V7X_SUBLANES)
        hg = u[rows, :] + a[rows, :] * carry
        carry = hg[0:1, :] if reverse else hg[V7X_SUBLANES - 1:V7X_SUBLANES, :]
        hs[g] = hg
    return jnp.concatenate(hs, axis=0)


def _lru_kernel(*refs, reverse, blocks, geom, tm, ntiles):
    if reverse:
        (x_ref, prev_ref, next_ref, cw_ref, cb_ref, wg_ref, br_ref, bi_ref, lg_ref,
         hf_ref, rg_ref, o_ref, carry_ref) = refs
    else:
        (x_ref, prev_ref, next_ref, cw_ref, cb_ref, wg_ref, br_ref, bi_ref, lg_ref,
         o_ref, carry_ref) = refs
    step = pl.program_id(0)
    i = (ntiles - 1 - step) if reverse else step
    pos0 = geom.pos0_of_tile(i, tm)
    at_start = pos0 == 0
    at_end = pos0 + tm == geom.seqlen_of_tile(i, tm)
    x = x_ref[...].astype(F32)
    row = lax.broadcasted_iota(I32, (tm, 1), 0)
    keep_prev = jnp.where(at_start, 0.0, 1.0)
    keep_next = jnp.where(at_end, 0.0, 1.0)
    prev = prev_ref[...].astype(F32)
    p6 = prev[HALO_ROWS - 2:HALO_ROWS - 1, :] * keep_prev
    p7 = prev[HALO_ROWS - 1:HALO_ROWS, :] * keep_prev
    n0 = next_ref[...].astype(F32)[0:1, :] * keep_next
    xm1 = jnp.where(row == 0, p7, pltpu.roll(x, 1, 0))
    xm2 = jnp.where(row == 0, p6, jnp.where(row == 1, p7, pltpu.roll(x, 2, 0)))
    xp1 = jnp.where(row == tm - 1, n0, pltpu.roll(x, tm - 1, 0))
    xc = cb_ref[...] + xm2 * cw_ref[0:1, :] + xm1 * cw_ref[1:2, :] + x * cw_ref[2:3, :] + xp1 * cw_ref[3:4, :]
    xcb = xc.astype(BF16)
    rs, is_ = [], []
    for n in range(blocks):
        cols = slice(n * V7X_LANES, (n + 1) * V7X_LANES)
        g = _dot(xcb[:, cols], wg_ref[n])
        rs.append(g[:, :V7X_LANES])
        is_.append(g[:, V7X_LANES:])
    r = _sigmoid(jnp.concatenate(rs, axis=1) + br_ref[...])
    ig = _sigmoid(jnp.concatenate(is_, axis=1) + bi_ref[...])
    lg = lg_ref[...]
    softplus = jnp.maximum(-lg, 0.0) + jnp.log(1.0 + jnp.exp(-jnp.abs(lg)))
    log_a = -LRU_C * r * softplus
    a = jnp.exp(log_a)
    mult = jnp.sqrt(1.0 - a * a)
    if reverse:
        mult = jnp.where(jnp.logical_and(at_end, row == tm - 1), 1.0, mult)
    else:
        mult = jnp.where(jnp.logical_and(at_start, row == 0), 1.0, mult)
    u = mult * (ig * xc)
    fresh = at_end if reverse else at_start
    carry = jnp.where(fresh, 0.0, carry_ref[0:1, :])
    h = _scan_tile(a, u, carry, reverse)
    if reverse:
        carry_ref[0:1, :] = h[0:1, :]
        o_ref[...] = ((hf_ref[...].astype(F32) + h) * _gelu(rg_ref[...].astype(F32))).astype(o_ref.dtype)
    else:
        carry_ref[0:1, :] = h[tm - 1:tm, :]
        o_ref[...] = h.astype(o_ref.dtype)


def _lru_pass(proj, xcol, gcol, conv_w, conv_b, wg, br, bi, lg, geom, width, reverse, h_fwd=None):
    T = proj.shape[0]
    tm = min(256, geom.unit)
    ntiles = T // tm
    blocks = width // V7X_LANES
    nb8 = T // HALO_ROWS
    t8 = tm // HALO_ROWS

    def tile(step):
        return (ntiles - 1 - step) if reverse else step

    in_specs = [pl.BlockSpec((tm, width), lambda s: (tile(s), xcol)),
                pl.BlockSpec((HALO_ROWS, width), lambda s: (jnp.maximum(tile(s) * t8 - 1, 0), xcol)),
                pl.BlockSpec((HALO_ROWS, width), lambda s: (jnp.minimum((tile(s) + 1) * t8, nb8 - 1), xcol)),
                pl.BlockSpec((4, width), lambda s: (0, 0)),
                pl.BlockSpec((1, width), lambda s: (0, 0)),
                pl.BlockSpec((blocks, V7X_LANES, 2 * V7X_LANES), lambda s: (0, 0, 0)),
                pl.BlockSpec((1, width), lambda s: (0, 0)),
                pl.BlockSpec((1, width), lambda s: (0, 0)),
                pl.BlockSpec((1, width), lambda s: (0, 0))]
    args = [proj, proj, proj, conv_w, conv_b.reshape(1, width), wg, br.reshape(1, width),
            bi.reshape(1, width), lg.reshape(1, width)]
    if reverse:
        in_specs += [pl.BlockSpec((tm, width), lambda s: (tile(s), 0)),
                     pl.BlockSpec((tm, width), lambda s: (tile(s), gcol))]
        args += [h_fwd, proj]
    return pl.pallas_call(
        functools.partial(_lru_kernel, reverse=reverse, blocks=blocks, geom=geom, tm=tm, ntiles=ntiles),
        out_shape=jax.ShapeDtypeStruct((T, width), BF16),
        grid=(ntiles,),
        in_specs=in_specs,
        out_specs=pl.BlockSpec((tm, width), lambda s: (tile(s), 0)),
        scratch_shapes=[pltpu.VMEM((V7X_SUBLANES, width), F32)],
        compiler_params=_params(("arbitrary",), 40 * 2**20),
        name="lru_bwd" if reverse else "lru_fwd",
    )(*args)


def _merge_kernel(oa_ref, ob_ref, oc_ref, gate_ref, w_ref, y_ref, acc_ref):
    n = pl.program_id(1)

    def branch(o_ref, first):
        p = _dot(o_ref[...], w_ref[...]) * _sigmoid(gate_ref[...].astype(F32))
        if first:
            acc_ref[...] = p
        else:
            acc_ref[...] += p

    @pl.when(n == 0)
    def _():
        branch(oa_ref, True)

    @pl.when(n == 1)
    def _():
        branch(ob_ref, False)

    @pl.when(n == 2)
    def _():
        branch(oc_ref, False)

    @pl.when(n == 3)
    def _():
        y_ref[...] = _dot(acc_ref[...].astype(BF16), w_ref[...]).astype(y_ref.dtype)


def _merge(oa, ob, oc, proj, gate_col0, w4, geom):
    T, d = oa.shape
    tm = min(512, geom.unit)
    return pl.pallas_call(
        _merge_kernel,
        out_shape=jax.ShapeDtypeStruct((T, d), BF16),
        grid=(T // tm, 4),
        in_specs=[pl.BlockSpec((tm, d), lambda i, n: (i, 0)),
                  pl.BlockSpec((tm, d), lambda i, n: (i, 0)),
                  pl.BlockSpec((tm, d), lambda i, n: (i, 0)),
                  pl.BlockSpec((tm, d), lambda i, n: (i, gate_col0 + jnp.minimum(n, 2))),
                  pl.BlockSpec((None, d, d), lambda i, n: (n, 0, 0))],
        out_specs=pl.BlockSpec((tm, d), lambda i, n: (i, 0)),
        scratch_shapes=[pltpu.VMEM((tm, d), F32)],
        compiler_params=_params(("arbitrary", "arbitrary"), 52 * 2**20),
        name="merge",
    )(oa, ob, oc, proj, w4)


def _pack_pairs(a):
    w = a.shape[1] // 2
    lo = lax.bitcast_convert_type(a[:, :w].astype(BF16).astype(F32), U32) >> 16
    hi = lax.bitcast_convert_type(a[:, w:].astype(BF16).astype(F32), U32) & np.uint32(0xFFFF0000)
    return lo | hi


def _unpack_pairs(p):
    lo = lax.bitcast_convert_type(p << 16, F32)
    hi = lax.bitcast_convert_type(p & np.uint32(0xFFFF0000), F32)
    return lo, hi


def _store_token_tiles(ref, base, packed):
    n, w = packed.shape
    nch = w // V7X_LANES
    for c in range(nch):
        ref[pl.ds(base + c, n, stride=nch), :] = packed[:, c * V7X_LANES:(c + 1) * V7X_LANES]


def _load_token_tiles(ref, base, n, nch):
    return [ref[pl.ds(base + c, n, stride=nch), :] for c in range(nch)]


def _res_mixer_kernel(x_ref, y_ref, gate_ref, g_ref, b_ref, sc_ref, sh_ref, wr_ref,
                      x1_ref, h_ref, hp_ref, lg_ref, *, alpha):
    z = alpha * x_ref[...] + gate_ref[...] * y_ref[...].astype(F32)
    x1 = _ln(z) * g_ref[...] + b_ref[...]
    x1_ref[...] = x1
    h = _ln(x1) * (1.0 + sc_ref[...]) + sh_ref[...]
    h_ref[...] = h.astype(h_ref.dtype)
    _store_token_tiles(hp_ref, 0, _pack_pairs(h))
    lg_ref[...] = _dot3_nt(wr_ref[...], h)


def _res_mixer(x, y, modr, ln_g, ln_b, w_router_t, geom, layer, alpha):
    T, d = x.shape
    ne = w_router_t.shape[0]
    tm = min(256, geom.unit)
    nch = d // 2 // V7X_LANES
    return pl.pallas_call(
        functools.partial(_res_mixer_kernel, alpha=alpha),
        out_shape=(jax.ShapeDtypeStruct((T, d), F32),
                   jax.ShapeDtypeStruct((T, d), BF16),
                   jax.ShapeDtypeStruct((T * nch, V7X_LANES), U32),
                   jax.ShapeDtypeStruct((ne, T), F32)),
        grid=(T // tm,),
        in_specs=[pl.BlockSpec((tm, d), lambda i: (i, 0)),
                  pl.BlockSpec((tm, d), lambda i: (i, 0)),
                  _mod_spec(geom, tm, layer, 2, d),
                  pl.BlockSpec((1, d), lambda i: (0, 0)),
                  pl.BlockSpec((1, d), lambda i: (0, 0)),
                  _mod_spec(geom, tm, layer, 4, d),
                  _mod_spec(geom, tm, layer, 3, d),
                  pl.BlockSpec((ne, d), lambda i: (0, 0))],
        out_specs=(pl.BlockSpec((tm, d), lambda i: (i, 0)),
                   pl.BlockSpec((tm, d), lambda i: (i, 0)),
                   pl.BlockSpec((tm * nch, V7X_LANES), lambda i: (i, 0)),
                   pl.BlockSpec((ne, tm), lambda i: (0, i))),
        compiler_params=_params(("arbitrary",), 40 * 2**20),
        name="res_mixer",
    )(x, y, modr, ln_g.reshape(1, d), ln_b.reshape(1, d), modr, modr, w_router_t)


def _res_moe_kernel(*refs, alpha, has_next):
    pair_refs, refs = refs[:TOP_K], refs[TOP_K:]
    if has_next:
        x_ref, sh_ref, gate_ref, g_ref, b_ref, sc_ref, shf_ref, x2_ref, h_ref = refs
    else:
        x_ref, sh_ref, gate_ref, g_ref, b_ref, x2_ref = refs
    tm, d = x_ref.shape
    nch = d // 2 // V7X_LANES
    los, his = [], []
    for c in range(nch):
        lo = hi = None
        for k in range(TOP_K):
            l, h = _unpack_pairs(pair_refs[k][pl.ds(c, tm, stride=nch), :])
            lo = l if lo is None else lo + l
            hi = h if hi is None else hi + h
        los.append(lo)
        his.append(hi)
    y = jnp.concatenate(los + his, axis=1) + sh_ref[...].astype(F32)
    z = alpha * x_ref[...] + gate_ref[...] * y
    x2 = _ln(z) * g_ref[...] + b_ref[...]
    x2_ref[...] = x2
    if has_next:
        h_ref[...] = (_ln(x2) * (1.0 + sc_ref[...]) + shf_ref[...]).astype(h_ref.dtype)


def _res_moe(x, pairbuf, shared, modr, ln_g, ln_b, geom, layer, alpha, has_next):
    T, d = x.shape
    tm = min(256, geom.unit)
    nch = d // 2 // V7X_LANES
    in_specs = [pl.BlockSpec((tm * nch, V7X_LANES), functools.partial(lambda k, i: (k * (T // tm) + i, 0), k))
                for k in range(TOP_K)]
    in_specs += [pl.BlockSpec((tm, d), lambda i: (i, 0)),
                 pl.BlockSpec((tm, d), lambda i: (i, 0)),
                 _mod_spec(geom, tm, layer, 5, d),
                 pl.BlockSpec((1, d), lambda i: (0, 0)),
                 pl.BlockSpec((1, d), lambda i: (0, 0))]
    args = [pairbuf] * TOP_K + [x, shared, modr, ln_g.reshape(1, d), ln_b.reshape(1, d)]
    out_shape = [jax.ShapeDtypeStruct((T, d), F32)]
    out_specs = [pl.BlockSpec((tm, d), lambda i: (i, 0))]
    if has_next:
        in_specs += [_mod_spec(geom, tm, layer + 1, 1, d), _mod_spec(geom, tm, layer + 1, 0, d)]
        args += [modr, modr]
        out_shape.append(jax.ShapeDtypeStruct((T, d), BF16))
        out_specs.append(pl.BlockSpec((tm, d), lambda i: (i, 0)))
    return pl.pallas_call(
        functools.partial(_res_moe_kernel, alpha=alpha, has_next=has_next),
        out_shape=tuple(out_shape),
        grid=(T // tm,),
        in_specs=in_specs,
        out_specs=tuple(out_specs),
        compiler_params=_params(("arbitrary",), 40 * 2**20),
        name="res_moe",
    )(*args)


def _argmax_rows(vals, row, nrow):
    m = jnp.max(vals, axis=0, keepdims=True)
    idx = jnp.min(jnp.where(vals == m, row, nrow), axis=0, keepdims=True)
    return m, idx


def _route_kernel(lg_ref, bias_ref, idx_ref, w_ref):
    ne, tr = lg_ref.shape
    gsz = ne // N_GROUPS
    scores = _sigmoid(lg_ref[...])
    biased = scores + bias_ref[...]
    neg = -jnp.inf
    grow = lax.broadcasted_iota(I32, (gsz, tr), 0)
    gscore = []
    for g in range(N_GROUPS):
        blk = biased[g * gsz:(g + 1) * gsz, :]
        m1, i1 = _argmax_rows(blk, grow, gsz)
        m2 = jnp.max(jnp.where(grow == i1, neg, blk), axis=0, keepdims=True)
        gscore.append(m1 + m2)
    gs = jnp.concatenate(gscore, axis=0)
    gidx = lax.broadcasted_iota(I32, (N_GROUPS, tr), 0)
    keep = jnp.zeros((N_GROUPS, tr), F32)
    for _ in range(TOPK_GROUPS):
        _, gi = _argmax_rows(gs, gidx, N_GROUPS)
        sel = gidx == gi
        keep = jnp.where(sel, 1.0, keep)
        gs = jnp.where(sel, neg, gs)
    keep_full = jnp.concatenate(
        [jnp.broadcast_to(keep[g:g + 1, :], (gsz, tr)) for g in range(N_GROUPS)], axis=0)
    masked = jnp.where(keep_full > 0.5, biased, neg)
    row = lax.broadcasted_iota(I32, (ne, tr), 0)
    idxs, ws = [], []
    for _ in range(TOP_K):
        _, ei = _argmax_rows(masked, row, ne)
        sel = row == ei
        ws.append(jnp.sum(jnp.where(sel, scores, 0.0), axis=0, keepdims=True))
        idxs.append(ei)
        masked = jnp.where(sel, neg, masked)
    wk = jnp.concatenate(ws, axis=0)
    idx_ref[...] = jnp.concatenate(idxs, axis=0)
    w_ref[...] = wk / jnp.sum(wk, axis=0, keepdims=True) * ROUTED_SCALE


def _route(logits_t, router_bias):
    ne, T = logits_t.shape
    tr = min(512, T)
    return pl.pallas_call(
        _route_kernel,
        out_shape=(jax.ShapeDtypeStruct((TOP_K, T), I32), jax.ShapeDtypeStruct((TOP_K, T), F32)),
        grid=(T // tr,),
        in_specs=[pl.BlockSpec((ne, tr), lambda i: (0, i)),
                  pl.BlockSpec((ne, 1), lambda i: (0, 0))],
        out_specs=(pl.BlockSpec((TOP_K, tr), lambda i: (0, i)),
                   pl.BlockSpec((TOP_K, tr), lambda i: (0, i))),
        compiler_params=_params(("arbitrary",), 32 * 2**20),
        name="route",
    )(logits_t, router_bias.reshape(ne, 1))


def _expert_kernel(be_ref, nact_ref,
                   tok0_ref, tok1_ref, tok2_ref, dst_ref, w_ref, x_hbm, wg_ref, wu_ref, wd_ref,
                   out_hbm,
                   xbuf, ybuf, xmat, wgb, wub, wdb, gsem, ssem, *, rows, nch):
    b = pl.program_id(0)
    nact = nact_ref[0]
    slot = b % 2
    xslot = b % 3
    span = rows * nch

    def gather_start(idx_ref, sl, inline=False):
        base = sl * span

        def issue(r):
            src = x_hbm.at[pl.ds(pl.multiple_of(idx_ref[0, r], nch), nch)]
            dst = xbuf.at[pl.ds(pl.multiple_of(base + r * nch, nch), nch)]
            pltpu.make_async_copy(src, dst, gsem.at[sl]).start()

        if inline:
            for r in range(rows):
                issue(r)
            return

        def body(t, c):
            for k in range(DMA_UNROLL):
                issue(t * DMA_UNROLL + k)
            return c
        lax.fori_loop(0, rows // DMA_UNROLL, body, 0)

    def gather_wait(sl):
        base = pl.multiple_of(sl * span, span)
        pltpu.make_async_copy(x_hbm.at[pl.ds(0, span)], xbuf.at[pl.ds(base, span)], gsem.at[sl]).wait()

    def scatter_start(sl):
        base = sl * span

        def body(t, c):
            for k in range(DMA_UNROLL):
                r = t * DMA_UNROLL + k
                src = ybuf.at[pl.ds(pl.multiple_of(base + r * nch, nch), nch)]
                dst = out_hbm.at[pl.ds(pl.multiple_of(dst_ref[0, r], nch), nch)]
                pltpu.make_async_copy(src, dst, ssem.at[sl]).start()
            return c
        lax.fori_loop(0, rows // DMA_UNROLL, body, 0)

    def scatter_wait(sl):
        base = pl.multiple_of(sl * span, span)
        pltpu.make_async_copy(ybuf.at[pl.ds(base, span)], out_hbm.at[pl.ds(0, span)], ssem.at[sl]).wait()

    @pl.when(jnp.logical_and(b == 0, nact > 0))
    def _():
        gather_start(tok0_ref, 0)

    @pl.when(jnp.logical_and(b == 0, nact > 0))
    def _():
        gather_start(tok1_ref, 1)

    @pl.when(b < nact)
    def _():
        prev_e = be_ref[jnp.maximum(b - 1, 0)]

        @pl.when(jnp.logical_or(b == 0, be_ref[b] != prev_e))
        def _():
            wgb[...] = wg_ref[...].astype(BF16)
            wub[...] = wu_ref[...].astype(BF16)
            wdb[...] = wd_ref[...].astype(BF16)

        @pl.when(b >= 2)
        def _():
            scatter_wait(slot)

        gather_wait(xslot)
        for c, p in enumerate(_load_token_tiles(xbuf, xslot * span, rows, nch)):
            lo, hi = _unpack_pairs(p)
            xmat[:, c * V7X_LANES:(c + 1) * V7X_LANES] = lo.astype(BF16)
            xmat[:, (nch + c) * V7X_LANES:(nch + c + 1) * V7X_LANES] = hi.astype(BF16)
        x = xmat[...]
        g = _dot(x, wgb[...])
        u = _dot(x, wub[...])
        wrow = w_ref[...]
        eye = lax.broadcasted_iota(I32, (rows, rows), 0) == lax.broadcasted_iota(I32, (rows, rows), 1)
        wcol = jnp.sum(jnp.where(eye, wrow, 0.0), axis=1, keepdims=True)
        hmid = (g * _sigmoid(g)) * (u * wcol)
        y = _dot(hmid.astype(BF16), wdb[...])
        gather_start(tok2_ref, (b + 2) % 3, inline=True)
        _store_token_tiles(ybuf, slot * span, _pack_pairs(y))
        scatter_start(slot)

        @pl.when(b == nact - 1)
        def _():
            scatter_wait(slot)
            gather_wait((b + 1) % 3)
            gather_wait((b + 2) % 3)

            @pl.when(b >= 1)
            def _():
                scatter_wait(1 - slot)


def _experts(hp, block_expert, nact, slot_tok, slot_dst, slot_w, w_e_gate, w_e_up, w_e_down, layer):
    d, de = w_e_gate.shape[-2:]
    nch = d // 2 // V7X_LANES
    T = hp.shape[0] // nch
    nblk = block_expert.shape[0]
    rows = EXPERT_ROWS
    tok3 = (slot_tok * nch).reshape(nblk, 1, rows)
    dst3 = (slot_dst * nch).reshape(nblk, 1, rows)
    w3 = slot_w.reshape(nblk, 1, rows)

    def emap(b, be, na):
        return (layer, be[jnp.minimum(b, jnp.maximum(na[0] - 1, 0))], 0, 0)

    def tok_spec(ahead):
        return pl.BlockSpec((None, 1, rows), lambda b, be, na: (jnp.minimum(b + ahead, nblk - 1), 0, 0),
                            memory_space=pltpu.SMEM)

    grid_spec = pltpu.PrefetchScalarGridSpec(
        num_scalar_prefetch=2,
        grid=(nblk,),
        in_specs=[tok_spec(0), tok_spec(1), tok_spec(2),
                  pl.BlockSpec((None, 1, rows), lambda b, be, na: (b, 0, 0), memory_space=pltpu.SMEM),
                  pl.BlockSpec((None, 1, rows), lambda b, be, na: (b, 0, 0)),
                  pl.BlockSpec(memory_space=pl.ANY),
                  pl.BlockSpec((None, None, d, de), emap),
                  pl.BlockSpec((None, None, d, de), emap),
                  pl.BlockSpec((None, None, de, d), emap)],
        out_specs=pl.BlockSpec(memory_space=pl.ANY),
        scratch_shapes=[pltpu.VMEM((3 * rows * nch, V7X_LANES), U32),
                        pltpu.VMEM((2 * rows * nch, V7X_LANES), U32),
                        pltpu.VMEM((rows, d), BF16),
                        pltpu.VMEM((d, de), BF16),
                        pltpu.VMEM((d, de), BF16),
                        pltpu.VMEM((de, d), BF16),
                        pltpu.SemaphoreType.DMA((3,)),
                        pltpu.SemaphoreType.DMA((2,))],
    )
    return pl.pallas_call(
        functools.partial(_expert_kernel, rows=rows, nch=nch),
        out_shape=jax.ShapeDtypeStruct(((TOP_K * T + 2 * rows) * nch, V7X_LANES), U32),
        grid_spec=grid_spec,
        compiler_params=_params(("arbitrary",), 52 * 2**20),
        name="experts",
    )(block_expert, nact, tok3, tok3, tok3, dst3, w3, hp, w_e_gate, w_e_up, w_e_down)


def _dispatch_plan(top_idx_t, top_w_t, n_experts):
    k, T = top_idx_t.shape
    rows = EXPERT_ROWS
    n_pairs = k * T
    pair_expert = top_idx_t.reshape(n_pairs)
    pair_w = top_w_t.reshape(n_pairs)
    sorted_e, sorted_pair, sorted_w = lax.sort(
        (pair_expert, jnp.arange(n_pairs, dtype=I32), pair_w), num_keys=1)
    experts = jnp.arange(n_experts, dtype=I32)
    start = jnp.sum(sorted_e[None, :] < experts[:, None], axis=1, dtype=I32)
    counts = jnp.concatenate([start[1:], jnp.full((1,), n_pairs, I32)]) - start
    padded = (counts + rows - 1) // rows * rows
    padded_end = jnp.cumsum(padded)
    padded_start = padded_end - padded
    nblk = n_pairs // rows + n_experts
    block_start = jnp.arange(nblk, dtype=I32) * rows
    block_expert = jnp.minimum(jnp.sum(padded_end[None, :] <= block_start[:, None], axis=1, dtype=I32),
                               n_experts - 1)
    nact = (padded_end[-1] // rows).astype(I32).reshape(1)
    off = block_start - padded_start[block_expert]
    r = jnp.arange(rows, dtype=I32)[None, :]
    j = off[:, None] + r
    valid = j < counts[block_expert][:, None]
    src = jnp.clip(start[block_expert][:, None] + j, 0, n_pairs - 1)
    slot_pair = jnp.where(valid, sorted_pair[src], 0)
    slot_tok = slot_pair % T
    slot_w = jnp.where(valid, sorted_w[src], 0.0)
    spare = n_pairs + (jnp.arange(nblk, dtype=I32)[:, None] % 2) * rows + r
    slot_dst = jnp.where(valid, slot_pair, spare)
    return block_expert, nact, slot_tok.astype(I32), slot_dst.astype(I32), slot_w.astype(F32)


def _shared_kernel(h_ref, wgu_ref, wd_ref, o_ref):
    gu = _dot(h_ref[...], wgu_ref[...])
    ds = gu.shape[1] // 2
    g, u = gu[:, :ds], gu[:, ds:]
    o_ref[...] = _dot(((g * _sigmoid(g)) * u).astype(BF16), wd_ref[...]).astype(o_ref.dtype)


def _shared(h, wgu, wd, geom):
    T, d = h.shape
    tm = min(1024, geom.unit)
    return pl.pallas_call(
        _shared_kernel,
        out_shape=jax.ShapeDtypeStruct((T, d), BF16),
        grid=(T // tm,),
        in_specs=[pl.BlockSpec((tm, d), lambda i: (i, 0)),
                  pl.BlockSpec(wgu.shape, lambda i: (0, 0)),
                  pl.BlockSpec(wd.shape, lambda i: (0, 0))],
        out_specs=pl.BlockSpec((tm, d), lambda i: (i, 0)),
        compiler_params=_params(("arbitrary",), 40 * 2**20),
        name="shared_expert",
    )(h, wgu, wd)


def _rope_tables(s):
    half = ROPE_DIM // 2
    inv_freq = ROPE_THETA ** (-jnp.arange(half, dtype=F32) / half)
    ang = jnp.arange(s, dtype=F32)[:, None] * inv_freq[None, :]
    cos, sin = jnp.cos(ang), jnp.sin(ang)
    z = jnp.zeros((s, V7X_LANES - ROPE_DIM), F32)
    cc = jnp.concatenate([cos, cos, z], axis=1)
    ss = jnp.concatenate([-sin, sin, z], axis=1)
    return cc, ss


def _swap_halves(w):
    half = w.shape[-1] // 2
    return jnp.concatenate([w[..., half:], w[..., :half]], axis=-1)


def kernel(x_prompt, x_sample, c_prompt, c_sample, w_mod, b_mod, w_in, q_norm_g, w_uq, kv_norm_g, w_ukv, sgu_norm_g, sgu_norm_b, w_spatial, b_spatial, conv_w, conv_b, w_rgate, b_rgate, w_igate, b_igate, lru_logit, w_branch, w_out, ln1_g, ln1_b, w_router, router_bias, w_e_gate, w_e_up, w_e_down, w_sh_gate, w_sh_up, w_sh_down, ln2_g, ln2_b):
    b1, s1, d = x_prompt.shape
    b2, s2, _ = x_sample.shape
    geom = _Geom(b1, s1, b2, s2)
    T = geom.T
    L = w_mod.shape[0]
    alpha = float((2 * L) ** 0.25)
    qlr = q_norm_g.shape[1]
    kvl = kv_norm_g.shape[1]
    heads = w_uq.shape[2] // (NOPE_DIM + ROPE_DIM)
    width = sgu_norm_g.shape[1]
    n_experts = w_router.shape[2]
    assert geom.nrows <= 8 and width == d and heads * V_DIM == d

    x = jnp.concatenate([x_prompt.reshape(b1 * s1, d), x_sample.reshape(b2 * s2, d)], axis=0)
    c8 = jnp.concatenate([c_prompt, c_sample, jnp.zeros((8 - geom.nrows, d), F32)], axis=0)
    modr = _mod_all(c8, w_mod, b_mod).reshape(L * 8 * 6, 1, d)
    cc, ss = _rope_tables(s1)

    o_q, o_kv, o_kr = 0, qlr, qlr + kvl
    o_gm = o_kr + ROPE_DIM
    o_rx = o_gm + 2 * width
    o_rg = o_rx + width
    o_gl = o_rg + width

    h = _lnmod(x, modr, geom, 0)
    for l in range(L):
        wl = w_in[l]
        w_qin = wl[:, o_q:o_kv].astype(BF16)
        w_kr = wl[:, o_kr:o_gm]
        w_kvin = jnp.concatenate([wl[:, o_kv:o_kr], w_kr, _swap_halves(w_kr)], axis=1).astype(BF16)
        w_rest = wl[:, o_gm:].astype(BF16)
        uq = w_uq[l].reshape(qlr, heads, NOPE_DIM + ROPE_DIM)
        wq_ext = jnp.concatenate([uq, _swap_halves(uq[:, :, NOPE_DIM:])], axis=2).reshape(qlr, heads * HEAD_PAD).astype(BF16)
        ukv = w_ukv[l].reshape(kvl, heads, NOPE_DIM + V_DIM)
        wk = ukv[:, :, :NOPE_DIM].reshape(kvl, heads * NOPE_DIM).astype(BF16)
        wv = ukv[:, :, NOPE_DIM:].reshape(kvl, heads * V_DIM).astype(BF16)

        q = _q_path(h, w_qin, q_norm_g[l], wq_ext, cc, ss, geom, heads)
        kk, vv = _kv_path(h, w_kvin, kv_norm_g[l], wk, wv, cc, ss, geom, heads)
        o_a = _attention(q, kk, vv, geom, heads)
        proj = _matmul(h, w_rest, min(1024, geom.unit), width // 2, name="in_proj")
        ws = w_spatial[l].astype(BF16)
        bs_full = jnp.repeat(b_spatial[l].T, V7X_LANES, axis=1)
        o_b = _sgu(proj, sgu_norm_g[l], sgu_norm_b[l], ws, bs_full, geom, width)
        xcol, gcol = 2, 3
        hf = None
        for dr in range(2):
            wg = jnp.concatenate([w_rgate[l, dr], w_igate[l, dr]], axis=2).astype(BF16)
            out = _lru_pass(proj, xcol, gcol, conv_w[l], conv_b[l], wg, b_rgate[l, dr], b_igate[l, dr],
                            lru_logit[l, dr], geom, width, reverse=(dr == 1), h_fwd=hf)
            hf = out
        o_c = hf
        w4 = jnp.concatenate([w_branch[l], w_out[l][None]], axis=0).astype(BF16)
        y = _merge(o_a, o_b, o_c, proj, 4, w4, geom)
        x, h2, hp, logits_t = _res_mixer(x, y, modr, ln1_g[l], ln1_b[l], w_router[l].T, geom, l, alpha)

        top_idx_t, top_w_t = _route(logits_t, router_bias[l])
        plan = _dispatch_plan(top_idx_t, top_w_t, n_experts)
        pairbuf = _experts(hp, *plan, w_e_gate, w_e_up, w_e_down, l)
        wgu = jnp.concatenate([w_sh_gate[l], w_sh_up[l]], axis=1).astype(BF16)
        shared = _shared(h2, wgu, w_sh_down[l].astype(BF16), geom)
        outs = _res_moe(x, pairbuf, shared, modr, ln2_g[l], ln2_b[l], geom, l, alpha, has_next=(l + 1 < L))
        if l + 1 < L:
            x, h = outs
        else:
            (x,) = outs

    y_prompt = x[:b1 * s1].reshape(b1, s1, d)
    y_sample = x[b1 * s1:].reshape(b2, s2, d)
    return (y_prompt, y_sample)
```
